```python
import jax, jax.numpy as jnp
from jax import lax
import numpy as np

D_MODEL = 1024
BATCH = 2
SEQ = 8192
DEPTH = 4

PLE_DIM = 256
RET_HEADS = 4
RET_HEAD_DIM = 128
SB_HEADS = 4
SB_HEAD_DIM = 128
RET_WIDTH = RET_HEADS * RET_HEAD_DIM
SB_WIDTH = SB_HEADS * SB_HEAD_DIM
EVEN_WIDTH = RET_WIDTH + SB_WIDTH
EVEN_PROJ = 4 * RET_WIDTH + 4 * SB_WIDTH
RET_CHUNK = 128
SB_BLOCK = 128
ROPE_BASE = 10000.0
RWKV_HEAD_DIM = 64
RWKV_HEADS = D_MODEL // RWKV_HEAD_DIM
DECAY_LORA = 64
ICLR_LORA = 64
VRES_LORA = 32
N_EVEN = (DEPTH + 1) // 2
N_ODD = DEPTH // 2
N_VRES = max(N_ODD - 1, 0)
RMS_EPS = 1e-6
GN_EPS = 1e-5
LNX_EPS = 64e-5

kernel_name = "hybrid_retention_stickbreak_rwkv7_ple"


def rmsnorm(x, gain):
    xf = x.astype(jnp.float32)
    y = xf * lax.rsqrt(jnp.mean(xf * xf, axis=-1, keepdims=True) + RMS_EPS)
    return (y * gain).astype(x.dtype)


def head_norm(y, eps):
    yf = y.astype(jnp.float32)
    mu = jnp.mean(yf, axis=-1, keepdims=True)
    var = jnp.mean(jnp.square(yf - mu), axis=-1, keepdims=True)
    out = (yf - mu) * lax.rsqrt(var + eps)
    return out.reshape(*y.shape[:-2], -1)


def rotary(x, pos):
    half = x.shape[-1] // 2
    inv_freq = ROPE_BASE ** (-jnp.arange(half, dtype=jnp.float32) / half)
    ang = pos.astype(jnp.float32)[:, None] * inv_freq[None, :]
    cos = jnp.cos(ang)[None, :, None, :]
    sin = jnp.sin(ang)[None, :, None, :]
    x1 = x[..., :half].astype(jnp.float32)
    x2 = x[..., half:].astype(jnp.float32)
    return jnp.concatenate([x1 * cos - x2 * sin, x1 * sin + x2 * cos], axis=-1).astype(x.dtype)


def retention(q, k, v, gn_gain):
    B, S, H, Dh = q.shape
    C = RET_CHUNK
    N = S // C
    pos = jnp.arange(S)
    q = rotary(q, pos)
    k = rotary(k, pos) * (Dh ** -0.5)
    lg = jnp.log(1.0 - 2.0 ** (-5.0 - jnp.arange(H, dtype=jnp.float32)))
    c = jnp.arange(C, dtype=jnp.float32)
    diff = c[:, None] - c[None, :]
    intra = jnp.where(diff[None] >= 0, jnp.exp(jnp.maximum(diff, 0.0)[None] * lg[:, None, None]), 0.0).astype(q.dtype)
    qc = q.reshape(B, N, C, H, Dh)
    kc = k.reshape(B, N, C, H, Dh)
    vc = v.reshape(B, N, C, H, Dh)
    scores = jnp.einsum('bnihd,bnjhd->bnhij', qc, kc) * intra
    out = jnp.einsum('bnhij,bnjhe->bnihe', scores, vc)
    k_dec = kc * jnp.exp((C - 1 - c)[:, None] * lg[None, :]).astype(q.dtype)[:, :, None]
    kv = jnp.einsum('bnjhd,bnjhe->nbhde', k_dec, vc)
    chunk_decay = jnp.exp(C * lg).astype(kv.dtype)[None, :, None, None]

    def carry_step(state, kv_n):
        return state * chunk_decay + kv_n, state

    _, prev = lax.scan(carry_step, jnp.zeros_like(kv[0]), kv)
    q_dec = qc * jnp.exp((c + 1)[:, None] * lg[None, :]).astype(q.dtype)[:, :, None]
    out = out + jnp.einsum('bnihd,nbhde->bnihe', q_dec, prev)
    out = out.reshape(B, S, H, Dh)
    return head_norm(out, GN_EPS) * gn_gain


def stick_breaking(q, k, v):
    B, S, H, Dh = q.shape
    nb = S // SB_BLOCK
    scale = Dh ** -0.5
    key_pos = jnp.arange(S)
    qb = q.reshape(B, nb, SB_BLOCK, H, Dh).transpose(1, 0, 2, 3, 4)

    def block(args):
        q_blk, b_idx = args
        z = jnp.einsum('bqhd,bkhd->bhqk', q_blk, k).astype(jnp.float32) * scale
        q_pos = b_idx * SB_BLOCK + jnp.arange(SB_BLOCK)
        mask = key_pos[None, :] < q_pos[:, None]
        log_1mb = jnp.where(mask, jax.nn.log_sigmoid(-z), 0.0)
        after = lax.cumsum(log_1mb, axis=3, reverse=True) - log_1mb
        w = jnp.where(mask, jnp.exp(jax.nn.log_sigmoid(z) + after), 0.0)
        return jnp.einsum('bhqk,bkhd->bqhd', w.astype(v.dtype), v)

    out = lax.map(block, (qb, jnp.arange(nb)))
    return out.transpose(1, 0, 2, 3, 4).reshape(B, S, H * Dh)


def even_mixer(hn, w_in, w_out, ret_gn_gain):
    B, S, _ = hn.shape
    z = jnp.einsum('bsd,df->bsf', hn, w_in)
    qa, ka, va, ga, qb, kb, vb, gb = jnp.split(z, 8, axis=-1)
    ra = lambda t: t.reshape(B, S, RET_HEADS, RET_HEAD_DIM)
    rb = lambda t: t.reshape(B, S, SB_HEADS, SB_HEAD_DIM)
    o_a = retention(ra(qa), ra(ka), ra(va), ret_gn_gain).astype(hn.dtype)
    o_b = stick_breaking(rb(qb), rb(kb), rb(vb))
    mixed = jnp.concatenate([o_a * jax.nn.silu(ga), o_b * jax.nn.silu(gb)], axis=-1)
    return jnp.einsum('bsf,fd->bsd', mixed, w_out)


def token_shift(x):
    return jnp.pad(x, ((0, 0), (1, 0), (0, 0)))[:, :-1]


def rwkv7_mixer(hn, mu, w_in, w_out, w0, w1, w2, a0, a1, a2, k_k, k_a, r_k, lnx_g, lnx_b, v_first, vres):
    B, S, D = hn.shape
    H, N = RWKV_HEADS, RWKV_HEAD_DIM
    xx = token_shift(hn) - hn
    xs = hn[None] + xx[None] * mu[:, None, None, :]
    r, k, v, g = jnp.einsum('pbsd,pdf->pbsf', xs[:4], w_in)
    xv, xw, xa = xs[2], xs[4], xs[5]
    w_log = -jax.nn.softplus(-(w0 + jnp.tanh(xw @ w1) @ w2)) - 0.5
    decay = jnp.exp(-jnp.exp(w_log.astype(jnp.float32)))
    a = jax.nn.sigmoid(a0 + (xa @ a1) @ a2)
    if v_first is None:
        v_first = v
    else:
        v0, v1, v2 = vres
        v = v + (v_first - v) * jax.nn.sigmoid(v0 + (xv @ v1) @ v2)
    kk = (k * k_k).reshape(B, S, H, N).astype(jnp.float32)
    kk = kk / jnp.maximum(jnp.linalg.norm(kk, axis=-1, keepdims=True), 1e-12)
    k = k * (1.0 + (a - 1.0) * k_a)
    heads = lambda t: t.reshape(B, S, H, N).astype(jnp.float32)
    rh, kh, vh, ah, wh = heads(r), heads(k), heads(v), heads(a), heads(decay)
    tm = lambda t: t.transpose(1, 0, 2, 3)

    def step(state, inp):
        r_t, w_t, k_t, v_t, kk_t, a_t = inp
        sk = jnp.einsum('bhvk,bhk->bhv', state, kk_t)
        state = state * w_t[:, :, None, :] - sk[..., None] * (kk_t * a_t)[:, :, None, :] + v_t[..., None] * k_t[:, :, None, :]
        return state, jnp.einsum('bhvk,bhk->bhv', state, r_t)

    s0 = jnp.zeros((B, H, N, N), jnp.float32)
    _, ys = lax.scan(step, s0, (tm(rh), tm(wh), tm(kh), tm(vh), tm(kk), tm(ah)))
    y = ys.transpose(1, 0, 2, 3)
    y = head_norm(y, LNX_EPS) * lnx_g + lnx_b
    bonus = jnp.sum(rh * kh * r_k, axis=-1, keepdims=True) * vh
    y = (y + bonus.reshape(B, S, D)).astype(hn.dtype)
    out = jnp.einsum('bsf,fd->bsd', y * jax.nn.silu(g), w_out)
    return out, v_first


def setup_inputs(seed: int = 0) -> dict:
    key = jax.random.key(seed)
    ks = jax.random.split(key, 32)
    nrm = lambda kk, shape, s: jax.random.normal(kk, shape, jnp.float32) * s
    uni = lambda kk, shape: jax.random.uniform(kk, shape, jnp.float32)
    D = D_MODEL
    return {
        "x": nrm(ks[0], (BATCH, SEQ, D), 1.0),
        "p": nrm(ks[1], (DEPTH, BATCH, SEQ, PLE_DIM), 1.0),
        "norm_gain": 1.0 + nrm(ks[2], (DEPTH, D), 0.1),
        "final_gain": 1.0 + nrm(ks[3], (D,), 0.1),
        "ple_proj": nrm(ks[4], (DEPTH, PLE_DIM, D), PLE_DIM ** -0.5),
        "ple_gate": nrm(ks[5], (DEPTH, D, D), D ** -0.5),
        "even_w_in": nrm(ks[6], (N_EVEN, D, EVEN_PROJ), D ** -0.5),
        "even_w_out": nrm(ks[7], (N_EVEN, EVEN_WIDTH, D), EVEN_WIDTH ** -0.5),
        "ret_gn_gain": 1.0 + nrm(ks[8], (N_EVEN, RET_WIDTH), 0.1),
        "odd_mu": uni(ks[9], (N_ODD, 6, D)),
        "odd_w_in": nrm(ks[10], (N_ODD, 4, D, D), D ** -0.5),
        "odd_w_out": nrm(ks[11], (N_ODD, D, D), D ** -0.5),
        "rwkv_w0": -6.0 + 5.0 * uni(ks[12], (N_ODD, D)),
        "rwkv_w1": nrm(ks[13], (N_ODD, D, DECAY_LORA), D ** -0.5),
        "rwkv_w2": nrm(ks[14], (N_ODD, DECAY_LORA, D), 0.1 * DECAY_LORA ** -0.5),
        "rwkv_a0": nrm(ks[15], (N_ODD, D), 0.1),
        "rwkv_a1": nrm(ks[16], (N_ODD, D, ICLR_LORA), D ** -0.5),
        "rwkv_a2": nrm(ks[17], (N_ODD, ICLR_LORA, D), 0.1 * ICLR_LORA ** -0.5),
        "rwkv_v0": 1.0 + nrm(ks[18], (N_VRES, D), 0.1),
        "rwkv_v1": nrm(ks[19], (N_VRES, D, VRES_LORA), D ** -0.5),
        "rwkv_v2": nrm(ks[20], (N_VRES, VRES_LORA, D), 0.1 * VRES_LORA ** -0.5),
        "rwkv_k_k": 0.85 + nrm(ks[21], (N_ODD, D), 0.1),
        "rwkv_k_a": 1.0 + nrm(ks[22], (N_ODD, D), 0.1),
        "rwkv_r_k": nrm(ks[23], (N_ODD, RWKV_HEADS, RWKV_HEAD_DIM), 0.1),
        "rwkv_lnx_gain": 1.0 + nrm(ks[24], (N_ODD, D), 0.1),
        "rwkv_lnx_bias": nrm(ks[25], (N_ODD, D), 0.02),
    }


def reference(x, p, norm_gain, final_gain, ple_proj, ple_gate, even_w_in, even_w_out, ret_gn_gain,
              odd_mu, odd_w_in, odd_w_out, rwkv_w0, rwkv_w1, rwkv_w2, rwkv_a0, rwkv_a1, rwkv_a2,
              rwkv_v0, rwkv_v1, rwkv_v2, rwkv_k_k, rwkv_k_a, rwkv_r_k, rwkv_lnx_gain, rwkv_lnx_bias):
    h = x
    v_first = None
    for i in range(DEPTH):
        hn = rmsnorm(h, norm_gain[i])
        if i % 2 == 0:
            e = i // 2
            h = h + even_mixer(hn, even_w_in[e], even_w_out[e], ret_gn_gain[e])
        else:
            o = i // 2
            vres = None if v_first is None else (rwkv_v0[o - 1], rwkv_v1[o - 1], rwkv_v2[o - 1])
            y, v_first = rwkv7_mixer(hn, odd_mu[o], odd_w_in[o], odd_w_out[o], rwkv_w0[o], rwkv_w1[o], rwkv_w2[o],
                                     rwkv_a0[o], rwkv_a1[o], rwkv_a2[o], rwkv_k_k[o], rwkv_k_a[o], rwkv_r_k[o],
                                     rwkv_lnx_gain[o], rwkv_lnx_bias[o], v_first, vres)
            h = h + y
        gate = jax.nn.sigmoid(jnp.einsum('bsd,de->bse', h, ple_gate[i]))
        h = h + gate * jnp.einsum('bsp,pd->bsd', p[i], ple_proj[i])
    return rmsnorm(h, final_gain)
```

```python
import functools

import jax
import jax.numpy as jnp
from jax import lax
from jax.experimental import pallas as pl
from jax.experimental.pallas import tpu as pltpu

F32 = jnp.float32
BF16 = jnp.bfloat16

LANES = 128
SUBLANES = 8
VMEM_LIMIT_BYTES = 56 * 1024 * 1024

RET_HEADS = 4
RET_HEAD_DIM = 128
SB_HEADS = 4
SB_HEAD_DIM = 128
ROPE_BASE = 10000.0
RWKV_HEAD_DIM = 64
RMS_EPS = 1e-6
GN_EPS = 1e-5
LNX_EPS = 64e-5

ROW_TILE = 512
ODD_ROW_TILE = 256
IN_COL_TILE = 1024
RET_CHUNK = 256
SB_Q_TILE = 256
SB_K_TILE = 128
SCAN_CHUNK = 64
SCAN_PAIRS = 4


def _bf(x):
    return x.astype(BF16)


def _dot(a, b):
    return jnp.dot(a, b, preferred_element_type=F32)


def _dot_nt(a, b):
    return lax.dot_general(a, b, (((1,), (1,)), ((), ())), preferred_element_type=F32)


def _dot_tn(a, b):
    return lax.dot_general(a, b, (((0,), (0,)), ((), ())), preferred_element_type=F32)


def _split(x):
    hi = _bf(x)
    lo = _bf(x - hi.astype(F32))
    return hi, lo


def _rms(x, gain):
    return x * lax.rsqrt(jnp.mean(x * x, axis=-1, keepdims=True) + RMS_EPS) * gain


def _softplus(y):
    return jnp.maximum(y, 0.0) + jnp.log(1.0 + jnp.exp(-jnp.abs(y)))


def _silu(g):
    return g * jax.nn.sigmoid(g)


def _params(*sem):
    return pltpu.CompilerParams(dimension_semantics=sem, vmem_limit_bytes=VMEM_LIMIT_BYTES)


def _even_in_kernel(h_ref, g_ref, w_ref, z_ref, hn_ref):
    @pl.when(pl.program_id(1) == 0)
    def _():
        hn_ref[...] = _bf(_rms(h_ref[...], g_ref[...]))

    z_ref[...] = _dot(hn_ref[...], w_ref[...])


def _even_in_proj(h, gain, w_in):
    t, d = h.shape
    f = w_in.shape[1]
    tm, tn = min(ROW_TILE, t), IN_COL_TILE
    return pl.pallas_call(
        _even_in_kernel,
        out_shape=jax.ShapeDtypeStruct((t, f), F32),
        grid=(t // tm, f // tn),
        in_specs=[
            pl.BlockSpec((tm, d), lambda i, j: (i, 0)),
            pl.BlockSpec((1, d), lambda i, j: (0, 0)),
            pl.BlockSpec((d, tn), lambda i, j: (0, j)),
        ],
        out_specs=pl.BlockSpec((tm, tn), lambda i, j: (i, j)),
        scratch_shapes=[pltpu.VMEM((tm, d), BF16)],
        compiler_params=_params("parallel", "arbitrary"),
        name="even_in_proj",
    )(h, gain.reshape(1, d), w_in)


def _ret_kernel(q_ref, k_ref, v_ref, g_ref, cos_ref, sin_ref, intra_ref, qdec_ref, kdec_ref, cdec_ref,
                gn_ref, o_ref, state_ref):
    @pl.when(pl.program_id(2) == 0)
    def _():
        state_ref[...] = jnp.zeros_like(state_ref)

    half = RET_HEAD_DIM // 2
    cos = cos_ref[...]
    sin = sin_ref[...]
    q = q_ref[...]
    k = k_ref[...]
    qr = q * cos + pltpu.roll(q, half, 1) * sin
    kr = (k * cos + pltpu.roll(k, half, 1) * sin) * (RET_HEAD_DIM ** -0.5)
    v = _bf(v_ref[...])
    scores = _dot_nt(_bf(qr), _bf(kr)) * intra_ref[...]
    state = state_ref[...]
    out = _dot(_bf(scores), v) + _dot(_bf(qr * qdec_ref[...]), _bf(state))
    state_ref[...] = state * cdec_ref[0:1, :] + _dot_tn(_bf(kr * kdec_ref[...]), v)
    mu = jnp.mean(out, axis=-1, keepdims=True)
    dev = out - mu
    var = jnp.mean(dev * dev, axis=-1, keepdims=True)
    y = dev * lax.rsqrt(var + GN_EPS) * gn_ref[...]
    o_ref[...] = _bf(y * _silu(g_ref[...]))


def _retention(z, batch, seq, gn_gain):
    t = z.shape[0]
    c = min(RET_CHUNK, seq)
    nc = seq // c
    hd = RET_HEAD_DIM
    half = hd // 2
    inv_freq = ROPE_BASE ** (-jnp.arange(half, dtype=F32) / half)
    ang = jnp.arange(seq, dtype=F32)[:, None] * inv_freq[None, :]
    cos2 = jnp.concatenate([jnp.cos(ang), jnp.cos(ang)], axis=-1)
    sin2 = jnp.concatenate([-jnp.sin(ang), jnp.sin(ang)], axis=-1)
    lg = jnp.log(1.0 - 2.0 ** (-5.0 - jnp.arange(RET_HEADS, dtype=F32)))
    ci = jnp.arange(c, dtype=F32)
    diff = ci[:, None] - ci[None, :]
    intra = jnp.where(diff[None] >= 0, jnp.exp(jnp.maximum(diff, 0.0)[None] * lg[:, None, None]), 0.0)
    qdec = jnp.broadcast_to(jnp.exp((ci + 1)[None, :, None] * lg[:, None, None]), (RET_HEADS, c, hd))
    kdec = jnp.broadcast_to(jnp.exp((c - 1 - ci)[None, :, None] * lg[:, None, None]), (RET_HEADS, c, hd))
    cdec = jnp.broadcast_to(jnp.exp(c * lg)[:, None, None], (RET_HEADS, SUBLANES, hd))

    def zcol(off):
        return pl.BlockSpec((c, hd), lambda b, h, n: (b * nc + n, off + h))

    return pl.pallas_call(
        _ret_kernel,
        out_shape=jax.ShapeDtypeStruct((t, RET_HEADS * hd), BF16),
        grid=(batch, RET_HEADS, nc),
        in_specs=[
            zcol(0), zcol(RET_HEADS), zcol(2 * RET_HEADS), zcol(3 * RET_HEADS),
            pl.BlockSpec((c, hd), lambda b, h, n: (n, 0)),
            pl.BlockSpec((c, hd), lambda b, h, n: (n, 0)),
            pl.BlockSpec((None, c, c), lambda b, h, n: (h, 0, 0)),
            pl.BlockSpec((None, c, hd), lambda b, h, n: (h, 0, 0)),
            pl.BlockSpec((None, c, hd), lambda b, h, n: (h, 0, 0)),
            pl.BlockSpec((None, SUBLANES, hd), lambda b, h, n: (h, 0, 0)),
            pl.BlockSpec((1, hd), lambda b, h, n: (0, h)),
        ],
        out_specs=pl.BlockSpec((c, hd), lambda b, h, n: (b * nc + n, h)),
        scratch_shapes=[pltpu.VMEM((hd, hd), F32)],
        compiler_params=_params("parallel", "parallel", "arbitrary"),
        name="retention",
    )(z, z, z, z, cos2, sin2, intra, qdec, kdec, cdec, gn_gain.reshape(1, -1))


def _sb_kernel(q_ref, k_ref, v_ref, g_ref, o_ref, acc_ref, run_ref, *, tq, tk):
    i = pl.program_id(2)
    scale = SB_HEAD_DIM ** -0.5
    q = _bf(q_ref[...])
    nd = tq // tk
    trow = lax.broadcasted_iota(jnp.int32, (tk, tk + LANES), 0)
    tcol = lax.broadcasted_iota(jnp.int32, (tk, tk + LANES), 1)
    tri = jnp.where((trow >= tcol) | (tcol >= tk), 1.0, 0.0).astype(BF16)
    qpos = i * tq + lax.broadcasted_iota(jnp.int32, (tq, tk), 0)
    kcol = lax.broadcasted_iota(jnp.int32, (tq, tk), 1)

    acc_ref[...] = jnp.zeros_like(acc_ref)
    run_ref[...] = jnp.zeros_like(run_ref)

    def block(kb, masked):
        ks = pl.multiple_of(kb * tk, tk)
        kblk = _bf(k_ref[pl.ds(ks, tk), :])
        vblk = _bf(v_ref[pl.ds(ks, tk), :])
        z = _dot_nt(q, kblk) * scale
        l1mb = -_softplus(z)
        if masked:
            mask = (ks + kcol) < qpos
            l1mb = jnp.where(mask, l1mb, 0.0)
        hi, lo = _split(l1mb)
        cs_all = _dot(hi, tri) + _dot(lo, tri)
        cs = cs_all[:, :tk]
        w = jnp.exp(z + cs + run_ref[...])
        if masked:
            w = jnp.where(mask, w, 0.0)
        acc_ref[...] += _dot(_bf(w), vblk)
        run_ref[...] += cs_all[:, tk:]

    for d in range(nd):
        block((i + 1) * nd - 1 - d, True)

    def body(j, carry):
        block(i * nd - 1 - j, False)
        return carry

    lax.fori_loop(0, i * nd, body, 0)
    o_ref[...] = _bf(acc_ref[...] * _silu(g_ref[...]))


def _stick_breaking(z, batch, seq):
    t = z.shape[0]
    hd = SB_HEAD_DIM
    tq = min(SB_Q_TILE, seq)
    tk = SB_K_TILE
    nq = seq // tq
    base = 4 * RET_HEADS
    kern = functools.partial(_sb_kernel, tq=tq, tk=tk)
    return pl.pallas_call(
        kern,
        out_shape=jax.ShapeDtypeStruct((t, SB_HEADS * hd), BF16),
        grid=(batch, SB_HEADS, nq),
        in_specs=[
            pl.BlockSpec((tq, hd), lambda b, h, i: (b * nq + i, base + h)),
            pl.BlockSpec((seq, hd), lambda b, h, i: (b, base + SB_HEADS + h)),
            pl.BlockSpec((seq, hd), lambda b, h, i: (b, base + 2 * SB_HEADS + h)),
            pl.BlockSpec((tq, hd), lambda b, h, i: (b * nq + i, base + 3 * SB_HEADS + h)),
        ],
        out_specs=pl.BlockSpec((tq, hd), lambda b, h, i: (b * nq + i, h)),
        scratch_shapes=[pltpu.VMEM((tq, hd), F32), pltpu.VMEM((tq, LANES), F32)],
        compiler_params=_params("parallel", "parallel", "arbitrary"),
        name="stick_breaking",
    )(z, z, z, z)


def _out_kernel(*refs, n_parts, final):
    h_ref = refs[0]
    m_refs = refs[1:1 + n_parts]
    w_refs = refs[1 + n_parts:1 + 2 * n_parts]
    p_ref, wg_ref, wp_ref = refs[1 + 2 * n_parts:4 + 2 * n_parts]
    rest = refs[4 + 2 * n_parts:]
    h1 = h_ref[...]
    for m_ref, w_ref in zip(m_refs, w_refs):
        h1 = h1 + _dot(m_ref[...], w_ref[...])
    gate = jax.nn.sigmoid(_dot(_bf(h1), wg_ref[...]))
    h2 = h1 + gate * _dot(_bf(p_ref[...]), wp_ref[...])
    if final:
        fg_ref, o_ref = rest
        o_ref[...] = _rms(h2, fg_ref[...])
    else:
        (o_ref,) = rest
        o_ref[...] = h2


def _out_proj_ple(h, parts, weights, p, w_gate, w_ple, final_gain=None):
    t, d = h.shape
    tm = min(ROW_TILE, t)
    final = final_gain is not None
    row = lambda width: pl.BlockSpec((tm, width), lambda i: (i, 0))
    full = lambda a: pl.BlockSpec(a.shape, lambda i: (0,) * a.ndim)
    args = [h, *parts, *weights, p, w_gate, w_ple]
    specs = [row(d)] + [row(m.shape[1]) for m in parts] + [full(w) for w in weights]
    specs += [row(p.shape[1]), full(w_gate), full(w_ple)]
    if final:
        args.append(final_gain.reshape(1, d))
        specs.append(full(args[-1]))
    return pl.pallas_call(
        functools.partial(_out_kernel, n_parts=len(parts), final=final),
        out_shape=jax.ShapeDtypeStruct((t, d), F32),
        grid=(t // tm,),
        in_specs=specs,
        out_specs=row(d),
        compiler_params=_params("parallel"),
        name="out_proj_ple",
    )(*args)


def _odd_in_kernel(*refs, vres, tiles_per_seq):
    (h_ref, hp_ref, gain_ref, mu_ref, win_ref, w0_ref, w1_ref, w2_ref, a0_ref, a1_ref, a2_ref,
     kk_ref, ka_ref) = refs[:13]
    if vres:
        v0_ref, v1_ref, v2_ref, vf_ref = refs[13:17]
        outs = refs[17:]
    else:
        outs = refs[13:]
    r_ref, lw_ref, k_ref, v_ref, kkr_ref, a_ref, g_ref = outs

    gain = gain_ref[...]
    hn = _rms(h_ref[...], gain)
    hnp = _rms(hp_ref[...], gain)
    first = (pl.program_id(0) % tiles_per_seq) == 0
    prev_row = jnp.where(first, 0.0, hnp[SUBLANES - 1:SUBLANES, :])
    rowid = lax.broadcasted_iota(jnp.int32, hn.shape, 0)
    shifted = jnp.where(rowid == 0, prev_row, pltpu.roll(hn, 1, 0))
    xx = shifted - hn
    mix = lambda p: hn + xx * mu_ref[p:p + 1, :]
    r = _dot(_bf(mix(0)), win_ref[0])
    k = _dot(_bf(mix(1)), win_ref[1])
    xv = _bf(mix(2))
    v = _dot(xv, win_ref[2])
    g = _dot(_bf(mix(3)), win_ref[3])
    xw = _bf(mix(4))
    xa = _bf(mix(5))
    w_log = -_softplus(-(w0_ref[...] + _dot(_bf(jnp.tanh(_dot(xw, w1_ref[...]))), w2_ref[...]))) - 0.5
    a = jax.nn.sigmoid(a0_ref[...] + _dot(_bf(_dot(xa, a1_ref[...])), a2_ref[...]))
    if vres:
        mixv = jax.nn.sigmoid(v0_ref[...] + _dot(_bf(_dot(xv, v1_ref[...])), v2_ref[...]))
        v = v + (vf_ref[...] - v) * mixv
    r_ref[...] = r
    lw_ref[...] = -jnp.exp(w_log)
    k_ref[...] = k * (1.0 + (a - 1.0) * ka_ref[...])
    v_ref[...] = v
    kkr_ref[...] = k * kk_ref[...]
    a_ref[...] = a
    g_ref[...] = g


def _odd_in_proj(h, seq, gain, mu, w_in, w0, w1, w2, a0, a1, a2, k_k, k_a, vres=None, v_first=None):
    t, d = h.shape
    tm = min(ODD_ROW_TILE, seq)
    tiles_per_seq = seq // tm
    sub_per_tile = tm // SUBLANES
    row = pl.BlockSpec((tm, d), lambda i: (i, 0))
    full = lambda a: pl.BlockSpec(a.shape, lambda i: (0,) * a.ndim)
    vec = lambda a: a.reshape(1, -1)
    args = [h, h, vec(gain), mu, w_in, vec(w0), w1, w2, vec(a0), a1, a2, vec(k_k), vec(k_a)]
    specs = [row, pl.BlockSpec((SUBLANES, d), lambda i: (jnp.maximum(i * sub_per_tile - 1, 0), 0))]
    specs += [full(a) for a in args[2:]]
    if vres is not None:
        v0, v1, v2 = vres
        extra = [vec(v0), v1, v2]
        args += extra + [v_first]
        specs += [full(a) for a in extra] + [row]
    return pl.pallas_call(
        functools.partial(_odd_in_kernel, vres=vres is not None, tiles_per_seq=tiles_per_seq),
        out_shape=[jax.ShapeDtypeStruct((t, d), F32)] * 7,
        grid=(t // tm,),
        in_specs=specs,
        out_specs=[row] * 7,
        compiler_params=_params("parallel"),
        name="odd_in_proj",
    )(*args)


def _scan_consts(c):
    row = lax.broadcasted_iota(jnp.int32, (c, LANES), 0)
    lane = lax.broadcasted_iota(jnp.int32, (c, LANES), 1)
    j = lane & (RWKV_HEAD_DIM - 1)
    lrow = lax.broadcasted_iota(jnp.int32, (c, c), 0)
    lcol = lax.broadcasted_iota(jnp.int32, (c, c), 1)
    brow = lax.broadcasted_iota(jnp.int32, (LANES, LANES), 0)
    bcol = lax.broadcasted_iota(jnp.int32, (LANES, LANES), 1)
    bd = (brow // RWKV_HEAD_DIM) == (bcol // RWKV_HEAD_DIM)
    return dict(
        strict=j < row, incl=j <= row, eye=j == row,
        blk16=(j >> 4) == (row >> 4), blk32=(j >> 5) == (row >> 5),
        lane_lo=lane < RWKV_HEAD_DIM,
        tril=jnp.where(lrow >= lcol, 1.0, 0.0).astype(BF16),
        bd=bd, bd_ones=jnp.where(bd, 1.0, 0.0).astype(BF16),
    )


def _pair_chunk(r, lw, k, v, kkraw, a, state, cs):
    c = r.shape[0]
    lane_lo = cs["lane_lo"]

    def stack(x):
        return jnp.concatenate([jnp.where(lane_lo, x, 0.0), jnp.where(lane_lo, 0.0, x)], axis=0)

    def pdot(x, y):
        return _dot(_bf(x), _bf(stack(y)))

    def headsum(x):
        hi, lo = _split(x)
        return _dot(hi, cs["bd_ones"]) + _dot(lo, cs["bd_ones"])

    hi, lo = _split(lw)
    cum = _dot(cs["tril"], hi) + _dot(cs["tril"], lo)
    w_in = jnp.exp(cum)
    w_ex = jnp.exp(cum - lw)
    w_inv = jnp.exp(-cum)
    wc = cum[c - 1:c, :]
    w_tail = jnp.exp(wc - cum)

    kk = kkraw * lax.rsqrt(jnp.maximum(headsum(kkraw * kkraw), 1e-24))
    b = kk * a
    rd = r * w_in
    kkd = kk * w_ex
    lhs2 = _bf(jnp.concatenate([kkd, rd], axis=0))
    ak = _dot_nt(lhs2, _bf(stack(k * w_inv)))
    ab = _dot_nt(lhs2, _bf(stack(b * w_inv)))
    a_kk = jnp.where(cs["strict"], ak[:c], 0.0)
    a_rk = jnp.where(cs["incl"], ak[c:], 0.0)
    n = jnp.where(cs["strict"], ab[:c], 0.0)
    a_rb = jnp.where(cs["incl"], ab[c:], 0.0)

    nd = jnp.where(cs["blk16"], n, 0.0)
    n2 = pdot(nd, nd)
    n4 = pdot(n2, n2)
    n8 = pdot(n4, n4)
    tinv = jnp.where(cs["eye"], 1.0, 0.0) - nd
    tinv = tinv + pdot(tinv, n2)
    tinv = tinv + pdot(tinv, n4)
    tinv = tinv + pdot(tinv, n8)
    off1 = jnp.where(cs["blk32"] & jnp.logical_not(cs["blk16"]), n, 0.0)
    tinv = tinv - pdot(tinv, pdot(off1, tinv))
    off2 = jnp.where(cs["blk32"], 0.0, n)
    tinv = tinv - pdot(tinv, pdot(off2, tinv))

    sb = _bf(state)
    vst = _bf(stack(v))
    gmat = _dot_nt(_bf(kkd), sb) + _dot(_bf(a_kk), vst)
    sk = pdot(tinv, gmat)
    y = _dot_nt(_bf(rd), sb) + _dot(_bf(a_rk), vst) - pdot(a_rb, sk)
    upd = _dot_tn(_bf(jnp.concatenate([v, sk], axis=0)),
                  _bf(jnp.concatenate([k * w_tail, -(b * w_tail)], axis=0)))
    state_new = state * jnp.exp(wc) + jnp.where(cs["bd"], upd, 0.0)
    return y, state_new, headsum


def _scan_kernel(r_ref, lw_ref, k_ref, v_ref, kkr_ref, a_ref, g_ref, rk_ref, lg_ref, lb_ref, o_ref, state_ref,
                 *, pairs):
    @pl.when(pl.program_id(2) == 0)
    def _():
        state_ref[...] = jnp.zeros_like(state_ref)

    c = r_ref.shape[0]
    cs = _scan_consts(c)
    inv_n = 1.0 / RWKV_HEAD_DIM
    for u in range(pairs):
        sl = slice(u * LANES, (u + 1) * LANES)
        r = r_ref[:, sl]
        k = k_ref[:, sl]
        v = v_ref[:, sl]
        y, state_new, headsum = _pair_chunk(r, lw_ref[:, sl], k, v, kkr_ref[:, sl], a_ref[:, sl],
                                            state_ref[u], cs)
        state_ref[u] = state_new
        mu = headsum(y) * inv_n
        dev = y - mu
        var = headsum(dev * dev) * inv_n
        yn = dev * lax.rsqrt(var + LNX_EPS) * lg_ref[:, sl] + lb_ref[:, sl]
        bonus = headsum(r * k * rk_ref[:, sl]) * v
        o_ref[:, sl] = _bf((yn + bonus) * _silu(g_ref[:, sl]))


def _rwkv_scan(r, lw, k, v, kkraw, a, g, batch, seq, r_k, lnx_g, lnx_b):
    t, d = r.shape
    c = SCAN_CHUNK
    nc = seq // c
    width = SCAN_PAIRS * LANES
    blk = pl.BlockSpec((c, width), lambda b, p, n: (b * nc + n, p))
    par = pl.BlockSpec((1, width), lambda b, p, n: (0, p))
    vec = lambda x: x.reshape(1, d)
    return pl.pallas_call(
        functools.partial(_scan_kernel, pairs=SCAN_PAIRS),
        out_shape=jax.ShapeDtypeStruct((t, d), BF16),
        grid=(batch, d // width, nc),
        in_specs=[blk] * 7 + [par] * 3,
        out_specs=blk,
        scratch_shapes=[pltpu.VMEM((SCAN_PAIRS, LANES, LANES), F32)],
        compiler_params=_params("parallel", "parallel", "arbitrary"),
        name="rwkv7_scan",
    )(r, lw, k, v, kkraw, a, g, vec(r_k), vec(lnx_g), vec(lnx_b))


def kernel(x, p, norm_gain, final_gain, ple_proj, ple_gate, even_w_in, even_w_out, ret_gn_gain, odd_mu, odd_w_in, odd_w_out, rwkv_w0, rwkv_w1, rwkv_w2, rwkv_a0, rwkv_a1, rwkv_a2, rwkv_v0, rwkv_v1, rwkv_v2, rwkv_k_k, rwkv_k_a, rwkv_r_k, rwkv_lnx_gain, rwkv_lnx_bias):
    batch, seq, d = x.shape
    depth = p.shape[0]
    t = batch * seq
    h = x.reshape(t, d)
    p2 = p.reshape(depth, t, p.shape[-1])
    ret_w = RET_HEADS * RET_HEAD_DIM
    v_first = None
    for i in range(depth):
        final = final_gain if i == depth - 1 else None
        if i % 2 == 0:
            e = i // 2
            z = _even_in_proj(h, norm_gain[i], _bf(even_w_in[e]))
            mixed_a = _retention(z, batch, seq, ret_gn_gain[e])
            mixed_b = _stick_breaking(z, batch, seq)
            w_out = _bf(even_w_out[e])
            parts, weights = [mixed_a, mixed_b], [w_out[:ret_w], w_out[ret_w:]]
        else:
            o = i // 2
            vres = None if v_first is None else (rwkv_v0[o - 1], _bf(rwkv_v1[o - 1]), _bf(rwkv_v2[o - 1]))
            r, lw, k, v, kkraw, a, g = _odd_in_proj(
                h, seq, norm_gain[i], odd_mu[o], _bf(odd_w_in[o]), rwkv_w0[o], _bf(rwkv_w1[o]), _bf(rwkv_w2[o]),
                rwkv_a0[o], _bf(rwkv_a1[o]), _bf(rwkv_a2[o]), rwkv_k_k[o], rwkv_k_a[o], vres, v_first)
            if v_first is None:
                v_first = v
            ymix = _rwkv_scan(r, lw, k, v, kkraw, a, g, batch, seq, rwkv_r_k[o], rwkv_lnx_gain[o],
                              rwkv_lnx_bias[o])
            parts, weights = [ymix], [_bf(odd_w_out[o])]
        h = _out_proj_ple(h, parts, weights, p2[i], _bf(ple_gate[i]), _bf(ple_proj[i]), final)
    return h.reshape(batch, seq, d)
```

```python
import functools

import jax
import jax.numpy as jnp
from jax import lax
from jax.experimental import pallas as pl
from jax.experimental.pallas import tpu as pltpu

F32 = jnp.float32
BF16 = jnp.bfloat16

LANES = 128
SUBLANES = 8
MXU_DIM = 256
VMEM_LIMIT_BYTES = 56 * 1024 * 1024

RET_HEADS = 4
RET_HEAD_DIM = 128
SB_HEADS = 4
SB_HEAD_DIM = 128
ROPE_BASE = 10000.0
RWKV_HEAD_DIM = 64
RMS_EPS = 1e-6
GN_EPS = 1e-5
LNX_EPS = 64e-5

ROW_TILE = 512
ODD_ROW_TILE = 256
IN_COL_TILE = 1024
RET_CHUNK = 256
SB_TILE = 512
SCAN_CHUNK = 64
HEADS_PER_GROUP = MXU_DIM // RWKV_HEAD_DIM


def _bf(x):
    return x.astype(BF16)


def _dot(a, b):
    return jnp.dot(a, b, preferred_element_type=F32)


def _dot_nt(a, b):
    return lax.dot_general(a, b, (((1,), (1,)), ((), ())), preferred_element_type=F32)


def _dot_tn(a, b):
    return lax.dot_general(a, b, (((0,), (0,)), ((), ())), preferred_element_type=F32)


def _split(x):
    hi = _bf(x)
    lo = _bf(x - hi.astype(F32))
    return hi, lo


def _rms(x, gain):
    return x * lax.rsqrt(jnp.mean(x * x, axis=-1, keepdims=True) + RMS_EPS) * gain


def _softplus(y):
    return jnp.maximum(y, 0.0) + jnp.log(1.0 + jnp.exp(-jnp.abs(y)))


def _silu(g):
    return g * jax.nn.sigmoid(g)


def _head_ones():
    r = lax.broadcasted_iota(jnp.int32, (MXU_DIM, MXU_DIM), 0)
    c = lax.broadcasted_iota(jnp.int32, (MXU_DIM, MXU_DIM), 1)
    return jnp.where((r // RWKV_HEAD_DIM) == (c // RWKV_HEAD_DIM), 1.0, 0.0).astype(BF16)


def _headsum(x, ones_bd):
    hi, lo = _split(x)
    cols = []
    for s in range(x.shape[1] // MXU_DIM):
        sl = slice(s * MXU_DIM, (s + 1) * MXU_DIM)
        cols.append(_dot(hi[:, sl], ones_bd) + _dot(lo[:, sl], ones_bd))
    return jnp.concatenate(cols, axis=1)


def _params(*sem):
    return pltpu.CompilerParams(dimension_semantics=sem, vmem_limit_bytes=VMEM_LIMIT_BYTES)


def _even_in_kernel(h_ref, g_ref, w_ref, z_ref, hn_ref):
    @pl.when(pl.program_id(1) == 0)
    def _():
        hn_ref[...] = _bf(_rms(h_ref[...], g_ref[...]))

    z_ref[...] = _dot(hn_ref[...], w_ref[...])


def _even_in_proj(h, gain, w_in):
    t, d = h.shape
    f = w_in.shape[1]
    tm, tn = min(ROW_TILE, t), IN_COL_TILE
    return pl.pallas_call(
        _even_in_kernel,
        out_shape=jax.ShapeDtypeStruct((t, f), F32),
        grid=(t // tm, f // tn),
        in_specs=[
            pl.BlockSpec((tm, d), lambda i, j: (i, 0)),
            pl.BlockSpec((1, d), lambda i, j: (0, 0)),
            pl.BlockSpec((d, tn), lambda i, j: (0, j)),
        ],
        out_specs=pl.BlockSpec((tm, tn), lambda i, j: (i, j)),
        scratch_shapes=[pltpu.VMEM((tm, d), BF16)],
        compiler_params=_params("parallel", "arbitrary"),
        name="even_in_proj",
    )(h, gain.reshape(1, d), w_in)


def _ret_kernel(q_ref, k_ref, v_ref, g_ref, cos_ref, sin_ref, intra_ref, qdec_ref, kdec_ref, cdec_ref,
                gn_ref, o_ref, state_ref):
    @pl.when(pl.program_id(2) == 0)
    def _():
        state_ref[...] = jnp.zeros_like(state_ref)

    half = RET_HEAD_DIM // 2
    cos = cos_ref[...]
    sin = sin_ref[...]
    q = q_ref[...]
    k = k_ref[...]
    qr = q * cos + pltpu.roll(q, half, 1) * sin
    kr = (k * cos + pltpu.roll(k, half, 1) * sin) * (RET_HEAD_DIM ** -0.5)
    v = _bf(v_ref[...])
    scores = _dot_nt(_bf(qr), _bf(kr)) * intra_ref[...]
    state = state_ref[...]
    out = _dot(_bf(scores), v) + _dot(_bf(qr * qdec_ref[...]), _bf(state))
    state_ref[...] = state * cdec_ref[0:1, :] + _dot_tn(_bf(kr * kdec_ref[...]), v)
    mu = jnp.mean(out, axis=-1, keepdims=True)
    dev = out - mu
    var = jnp.mean(dev * dev, axis=-1, keepdims=True)
    y = dev * lax.rsqrt(var + GN_EPS) * gn_ref[...]
    o_ref[...] = _bf(y * _silu(g_ref[...]))


def _retention(z, batch, seq, gn_gain):
    t = z.shape[0]
    c = min(RET_CHUNK, seq)
    nc = seq // c
    hd = RET_HEAD_DIM
    half = hd // 2
    inv_freq = ROPE_BASE ** (-jnp.arange(half, dtype=F32) / half)
    ang = jnp.arange(seq, dtype=F32)[:, None] * inv_freq[None, :]
    cos2 = jnp.concatenate([jnp.cos(ang), jnp.cos(ang)], axis=-1)
    sin2 = jnp.concatenate([-jnp.sin(ang), jnp.sin(ang)], axis=-1)
    lg = jnp.log(1.0 - 2.0 ** (-5.0 - jnp.arange(RET_HEADS, dtype=F32)))
    ci = jnp.arange(c, dtype=F32)
    diff = ci[:, None] - ci[None, :]
    intra = jnp.where(diff[None] >= 0, jnp.exp(jnp.maximum(diff, 0.0)[None] * lg[:, None, None]), 0.0)
    qdec = jnp.broadcast_to(jnp.exp((ci + 1)[None, :, None] * lg[:, None, None]), (RET_HEADS, c, hd))
    kdec = jnp.broadcast_to(jnp.exp((c - 1 - ci)[None, :, None] * lg[:, None, None]), (RET_HEADS, c, hd))
    cdec = jnp.broadcast_to(jnp.exp(c * lg)[:, None, None], (RET_HEADS, SUBLANES, hd))

    def zcol(off):
        return pl.BlockSpec((c, hd), lambda b, h, n: (b * nc + n, off + h))

    return pl.pallas_call(
        _ret_kernel,
        out_shape=jax.ShapeDtypeStruct((t, RET_HEADS * hd), BF16),
        grid=(batch, RET_HEADS, nc),
        in_specs=[
            zcol(0), zcol(RET_HEADS), zcol(2 * RET_HEADS), zcol(3 * RET_HEADS),
            pl.BlockSpec((c, hd), lambda b, h, n: (n, 0)),
            pl.BlockSpec((c, hd), lambda b, h, n: (n, 0)),
            pl.BlockSpec((None, c, c), lambda b, h, n: (h, 0, 0)),
            pl.BlockSpec((None, c, hd), lambda b, h, n: (h, 0, 0)),
            pl.BlockSpec((None, c, hd), lambda b, h, n: (h, 0, 0)),
            pl.BlockSpec((None, SUBLANES, hd), lambda b, h, n: (h, 0, 0)),
            pl.BlockSpec((1, hd), lambda b, h, n: (0, h)),
        ],
        out_specs=pl.BlockSpec((c, hd), lambda b, h, n: (b * nc + n, h)),
        scratch_shapes=[pltpu.VMEM((hd, hd), F32)],
        compiler_params=_params("parallel", "parallel", "arbitrary"),
        name="retention",
    )(z, z, z, z, cos2, sin2, intra, qdec, kdec, cdec, gn_gain.reshape(1, -1))


def _sb_kernel(q_ref, k_ref, v_ref, g_ref, o_ref, acc_ref, run_ref, *, tile):
    i = pl.program_id(2)
    scale = SB_HEAD_DIM ** -0.5
    q = _bf(q_ref[...])
    grp = MXU_DIM
    ngrp = tile // grp
    trow = lax.broadcasted_iota(jnp.int32, (grp, grp), 0)
    tcol = lax.broadcasted_iota(jnp.int32, (grp, grp), 1)
    tri = jnp.where(trow >= tcol, 1.0, 0.0).astype(BF16)

    acc_ref[...] = jnp.zeros_like(acc_ref)
    run_ref[...] = jnp.zeros_like(run_ref)

    def span(kb, diagonal):
        ks = pl.multiple_of(kb * tile, tile)
        kblk = _bf(k_ref[pl.ds(ks, tile), :])
        vblk = _bf(v_ref[pl.ds(ks, tile), :])
        z = _dot_nt(q, kblk) * scale
        l1mb = -_softplus(z)
        if diagonal:
            mask = (lax.broadcasted_iota(jnp.int32, (tile, tile), 1)
                    < lax.broadcasted_iota(jnp.int32, (tile, tile), 0))
            l1mb = jnp.where(mask, l1mb, 0.0)
        hi, lo = _split(l1mb)
        run = run_ref[...]
        ws = [None] * ngrp
        for gi in reversed(range(ngrp)):
            sl = slice(gi * grp, (gi + 1) * grp)
            cs = _dot(hi[:, sl], tri) + _dot(lo[:, sl], tri)
            tot = z[:, sl] + cs + jnp.concatenate([run] * (grp // LANES), axis=1)
            w = jnp.exp(tot)
            if diagonal:
                w = jnp.where(mask[:, sl], w, 0.0)
            ws[gi] = _bf(w)
            run = run + jnp.broadcast_to(cs[:, 0:1], run.shape)
        acc_ref[...] += _dot(jnp.concatenate(ws, axis=1), vblk)
        run_ref[...] = run

    span(i, True)

    def body(j, carry):
        span(i - 1 - j, False)
        return carry

    lax.fori_loop(0, i, body, 0)
    o_ref[...] = _bf(acc_ref[...] * _silu(g_ref[...]))


def _stick_breaking(z, batch, seq):
    t = z.shape[0]
    hd = SB_HEAD_DIM
    tile = min(SB_TILE, seq)
    nq = seq // tile
    base = 4 * RET_HEADS
    return pl.pallas_call(
        functools.partial(_sb_kernel, tile=tile),
        out_shape=jax.ShapeDtypeStruct((t, SB_HEADS * hd), BF16),
        grid=(batch, SB_HEADS, nq),
        in_specs=[
            pl.BlockSpec((tile, hd), lambda b, h, i: (b * nq + i, base + h)),
            pl.BlockSpec((seq, hd), lambda b, h, i: (b, base + SB_HEADS + h)),
            pl.BlockSpec((seq, hd), lambda b, h, i: (b, base + 2 * SB_HEADS + h)),
            pl.BlockSpec((tile, hd), lambda b, h, i: (b * nq + i, base + 3 * SB_HEADS + h)),
        ],
        out_specs=pl.BlockSpec((tile, hd), lambda b, h, i: (b * nq + i, h)),
        scratch_shapes=[pltpu.VMEM((tile, hd), F32), pltpu.VMEM((tile, LANES), F32)],
        compiler_params=_params("parallel", "parallel", "arbitrary"),
        name="stick_breaking",
    )(z, z, z, z)


def _ple_tail(h1, p_ref, wg_ref, wp_ref, rest, final):
    gate = jax.nn.sigmoid(_dot(_bf(h1), wg_ref[...]))
    h2 = h1 + gate * _dot(_bf(p_ref[...]), wp_ref[...])
    if final:
        fg_ref, o_ref = rest
        o_ref[...] = _rms(h2, fg_ref[...])
    else:
        (o_ref,) = rest
        o_ref[...] = h2


def _even_out_kernel(h_ref, ma_ref, mb_ref, wa_ref, wb_ref, p_ref, wg_ref, wp_ref, *rest, final):
    h1 = h_ref[...] + _dot(ma_ref[...], wa_ref[...]) + _dot(mb_ref[...], wb_ref[...])
    _ple_tail(h1, p_ref, wg_ref, wp_ref, rest, final)


def _odd_out_kernel(h_ref, y_ref, bonus_ref, gs_ref, lg_ref, lb_ref, w_ref, p_ref, wg_ref, wp_ref, *rest, final):
    ones_bd = _head_ones()
    inv_n = 1.0 / RWKV_HEAD_DIM
    y = y_ref[...]
    dev = y - _headsum(y, ones_bd) * inv_n
    var = _headsum(dev * dev, ones_bd) * inv_n
    yn = dev * lax.rsqrt(var + LNX_EPS) * lg_ref[...] + lb_ref[...]
    mixed = _bf((yn + bonus_ref[...]) * gs_ref[...])
    _ple_tail(h_ref[...] + _dot(mixed, w_ref[...]), p_ref, wg_ref, wp_ref, rest, final)


def _out_call(kern, name, h, row_args, full_args_mid, p, w_gate, w_ple, final_gain):
    t, d = h.shape
    tm = min(ROW_TILE, t)
    final = final_gain is not None
    row = lambda a: pl.BlockSpec((tm, a.shape[1]), lambda i: (i, 0))
    full = lambda a: pl.BlockSpec(a.shape, lambda i: (0,) * a.ndim)
    args = [h, *row_args, *full_args_mid, p, w_gate, w_ple]
    specs = [row(h)] + [row(a) for a in row_args] + [full(a) for a in full_args_mid] + [row(p), full(w_gate),
                                                                                        full(w_ple)]
    if final:
        args.append(final_gain.reshape(1, d))
        specs.append(full(args[-1]))
    return pl.pallas_call(
        functools.partial(kern, final=final),
        out_shape=jax.ShapeDtypeStruct((t, d), F32),
        grid=(t // tm,),
        in_specs=specs,
        out_specs=row(h),
        compiler_params=_params("parallel"),
        name=name,
    )(*args)


def _odd_in_kernel(*refs, vres, tiles_per_seq, chunk):
    (h_ref, hp_ref, gain_ref, mu_ref, win_ref, w0_ref, w1_ref, w2_ref, a0_ref, a1_ref, a2_ref,
     kk_ref, ka_ref, rk_ref) = refs[:14]
    if vres:
        v0_ref, v1_ref, v2_ref, vf_ref = refs[14:18]
        outs = refs[18:]
    else:
        outs = refs[14:]
    rd_ref, kkd_ref, kinv_ref, binv_ref, kt_ref, bt_ref, vb_ref, wc_ref, bonus_ref, gs_ref = outs[:10]

    gain = gain_ref[...]
    hn = _rms(h_ref[...], gain)
    hnp = _rms(hp_ref[...], gain)
    tm = hn.shape[0]
    first = (pl.program_id(0) % tiles_per_seq) == 0
    prev_row = jnp.where(first, 0.0, hnp[SUBLANES - 1:SUBLANES, :])
    rowid = lax.broadcasted_iota(jnp.int32, hn.shape, 0)
    shifted = jnp.where(rowid == 0, prev_row, pltpu.roll(hn, 1, 0))
    xx = shifted - hn
    mix = lambda p: hn + xx * mu_ref[p:p + 1, :]
    r = _dot(_bf(mix(0)), win_ref[0])
    k = _dot(_bf(mix(1)), win_ref[1])
    xv = _bf(mix(2))
    v = _dot(xv, win_ref[2])
    g = _dot(_bf(mix(3)), win_ref[3])
    xw = _bf(mix(4))
    xa = _bf(mix(5))
    w_log = -_softplus(-(w0_ref[...] + _dot(_bf(jnp.tanh(_dot(xw, w1_ref[...]))), w2_ref[...]))) - 0.5
    lw = -jnp.exp(w_log)
    a = jax.nn.sigmoid(a0_ref[...] + _dot(_bf(_dot(xa, a1_ref[...])), a2_ref[...]))
    if vres:
        mixv = jax.nn.sigmoid(v0_ref[...] + _dot(_bf(_dot(xv, v1_ref[...])), v2_ref[...]))
        v = v + (vf_ref[...] - v) * mixv
    else:
        outs[10][...] = v
    ones_bd = _head_ones()
    kkraw = k * kk_ref[...]
    kk = kkraw * lax.rsqrt(jnp.maximum(_headsum(kkraw * kkraw, ones_bd), 1e-24))
    kmod = k * (1.0 + (a - 1.0) * ka_ref[...])
    b = kk * a

    ri = lax.broadcasted_iota(jnp.int32, (tm, tm), 0)
    ci = lax.broadcasted_iota(jnp.int32, (tm, tm), 1)
    same = (ri // chunk) == (ci // chunk)
    sel = jnp.concatenate([jnp.where(same & (ri >= ci), 1.0, 0.0), jnp.where(same, 1.0, 0.0)], axis=0).astype(BF16)
    hi, lo = _split(lw)
    both = _dot(sel, hi) + _dot(sel, lo)
    cum = both[:tm]
    wc = jnp.exp(both[tm:])
    w_inv = jnp.exp(-cum)
    kinv = kmod * w_inv
    binv = b * w_inv
    rd_ref[...] = _bf(r * jnp.exp(cum))
    kkd_ref[...] = _bf(kk * jnp.exp(cum - lw))
    kinv_ref[...] = _bf(kinv)
    binv_ref[...] = _bf(binv)
    kt_ref[...] = _bf(kinv * wc)
    bt_ref[...] = _bf(-(binv * wc))
    vb_ref[...] = _bf(v)
    for c in range(tm // chunk):
        wc_ref[c] = wc[c * chunk:c * chunk + 1, :]
    bonus_ref[...] = _headsum(r * kmod * rk_ref[...], ones_bd) * v
    gs_ref[...] = _silu(g)


def _odd_in_proj(h, seq, gain, mu, w_in, w0, w1, w2, a0, a1, a2, k_k, k_a, r_k, vres=None, v_first=None):
    t, d = h.shape
    tm = min(ODD_ROW_TILE, seq)
    chunk = SCAN_CHUNK
    tiles_per_seq = seq // tm
    sub_per_tile = tm // SUBLANES
    row = pl.BlockSpec((tm, d), lambda i: (i, 0))
    full = lambda a: pl.BlockSpec(a.shape, lambda i: (0,) * a.ndim)
    vec = lambda a: a.reshape(1, -1)
    args = [h, h, vec(gain), mu, w_in, vec(w0), w1, w2, vec(a0), a1, a2, vec(k_k), vec(k_a), vec(r_k)]
    specs = [row, pl.BlockSpec((SUBLANES, d), lambda i: (jnp.maximum(i * sub_per_tile - 1, 0), 0))]
    specs += [full(a) for a in args[2:]]
    if vres is not None:
        v0, v1, v2 = vres
        extra = [vec(v0), v1, v2]
        args += extra + [v_first]
        specs += [full(a) for a in extra] + [row]
    act = lambda dt: jax.ShapeDtypeStruct((t, d), dt)
    out_shape = [act(BF16)] * 7 + [jax.ShapeDtypeStruct((t // chunk, 1, d), F32), act(F32), act(F32)]
    out_specs = [row] * 7 + [pl.BlockSpec((tm // chunk, 1, d), lambda i: (i, 0, 0)), row, row]
    if vres is None:
        out_shape.append(act(F32))
        out_specs.append(row)
    return pl.pallas_call(
        functools.partial(_odd_in_kernel, vres=vres is not None, tiles_per_seq=tiles_per_seq, chunk=chunk),
        out_shape=out_shape,
        grid=(t // tm,),
        in_specs=specs,
        out_specs=out_specs,
        compiler_params=_params("parallel"),
        name="odd_in_proj",
    )(*args)


def _scan_kernel(rd_ref, kkd_ref, kinv_ref, binv_ref, kt_ref, bt_ref, v_ref, wc_ref, y_ref, state_ref, *, groups):
    @pl.when(pl.program_id(1) == 0)
    def _():
        state_ref[...] = jnp.zeros_like(state_ref)

    c = rd_ref.shape[0]
    w = MXU_DIM
    hd = RWKV_HEAD_DIM
    row = lax.broadcasted_iota(jnp.int32, (c, w), 0)
    lane = lax.broadcasted_iota(jnp.int32, (c, w), 1)
    j = lane & (hd - 1)
    strict, incl, eye = j < row, j <= row, j == row
    blk16 = (j >> 4) == (row >> 4)
    blk32 = (j >> 5) == (row >> 5)
    head_of_lane = lane // hd
    head_masks = [head_of_lane == hh for hh in range(HEADS_PER_GROUP)]
    brow = lax.broadcasted_iota(jnp.int32, (w, w), 0)
    bcol = lax.broadcasted_iota(jnp.int32, (w, w), 1)
    bd = (brow // hd) == (bcol // hd)
    zero = jnp.zeros((), BF16)

    def stack(x):
        return jnp.concatenate([jnp.where(m, x, zero) for m in head_masks], axis=0)

    def pdot(x, y):
        return _dot(_bf(x), stack(_bf(y)))

    gs = range(groups)
    sl = [slice(g * w, (g + 1) * w) for g in gs]
    rd = [rd_ref[:, s] for s in sl]
    kkd = [kkd_ref[:, s] for s in sl]
    v = [v_ref[:, s] for s in sl]
    lhs2 = [jnp.concatenate([kkd[g], rd[g]], axis=0) for g in gs]
    ak = [_dot_nt(lhs2[g], stack(kinv_ref[:, sl[g]])) for g in gs]
    ab = [_dot_nt(lhs2[g], stack(binv_ref[:, sl[g]])) for g in gs]
    a_kk = [_bf(jnp.where(strict, ak[g][:c], 0.0)) for g in gs]
    a_rk = [_bf(jnp.where(incl, ak[g][c:], 0.0)) for g in gs]
    n = [jnp.where(strict, ab[g][:c], 0.0) for g in gs]
    a_rb = [jnp.where(incl, ab[g][c:], 0.0) for g in gs]

    nd = [jnp.where(blk16, n[g], 0.0) for g in gs]
    n2 = [pdot(nd[g], nd[g]) for g in gs]
    n4 = [pdot(n2[g], n2[g]) for g in gs]
    n8 = [pdot(n4[g], n4[g]) for g in gs]
    tinv = [jnp.where(eye, 1.0, 0.0) - nd[g] for g in gs]
    tinv = [tinv[g] + pdot(tinv[g], n2[g]) for g in gs]
    tinv = [tinv[g] + pdot(tinv[g], n4[g]) for g in gs]
    tinv = [tinv[g] + pdot(tinv[g], n8[g]) for g in gs]
    off1 = [jnp.where(blk32 & jnp.logical_not(blk16), n[g], 0.0) for g in gs]
    m1 = [pdot(off1[g], tinv[g]) for g in gs]
    tinv = [tinv[g] - pdot(tinv[g], m1[g]) for g in gs]
    off2 = [jnp.where(blk32, 0.0, n[g]) for g in gs]
    m2 = [pdot(off2[g], tinv[g]) for g in gs]
    tinv = [tinv[g] - pdot(tinv[g], m2[g]) for g in gs]

    state = [state_ref[g] for g in gs]
    sb = [_bf(state[g]) for g in gs]
    vst = [stack(v[g]) for g in gs]
    gmat = [_dot_nt(kkd[g], sb[g]) + _dot(a_kk[g], vst[g]) for g in gs]
    sk = [pdot(tinv[g], gmat[g]) for g in gs]
    skb = [_bf(sk[g]) for g in gs]
    for g in gs:
        y_ref[:, sl[g]] = _dot_nt(rd[g], sb[g]) + _dot(a_rk[g], vst[g]) - _dot(_bf(a_rb[g]), stack(skb[g]))
    for g in gs:
        upd = _dot_tn(jnp.concatenate([v[g], skb[g]], axis=0),
                      jnp.concatenate([kt_ref[:, sl[g]], bt_ref[:, sl[g]]], axis=0))
        state_ref[g] = state[g] * wc_ref[0, :, sl[g]] + jnp.where(bd, upd, 0.0)


def _rwkv_scan(rd, kkd, kinv, binv, ktail, btail, vb, wc, batch, seq):
    t, d = rd.shape
    c = SCAN_CHUNK
    nc = seq // c
    groups = d // MXU_DIM
    blk = pl.BlockSpec((c, d), lambda b, n: (b * nc + n, 0))
    return pl.pallas_call(
        functools.partial(_scan_kernel, groups=groups),
        out_shape=jax.ShapeDtypeStruct((t, d), F32),
        grid=(batch, nc),
        in_specs=[blk] * 7 + [pl.BlockSpec((1, 1, d), lambda b, n: (b * nc + n, 0, 0))],
        out_specs=blk,
        scratch_shapes=[pltpu.VMEM((groups, MXU_DIM, MXU_DIM), F32)],
        compiler_params=_params("parallel", "arbitrary"),
        name="rwkv7_scan",
    )(rd, kkd, kinv, binv, ktail, btail, vb, wc)


def kernel(x, p, norm_gain, final_gain, ple_proj, ple_gate, even_w_in, even_w_out, ret_gn_gain, odd_mu, odd_w_in, odd_w_out, rwkv_w0, rwkv_w1, rwkv_w2, rwkv_a0, rwkv_a1, rwkv_a2, rwkv_v0, rwkv_v1, rwkv_v2, rwkv_k_k, rwkv_k_a, rwkv_r_k, rwkv_lnx_gain, rwkv_lnx_bias):
    batch, seq, d = x.shape
    depth = p.shape[0]
    t = batch * seq
    h = x.reshape(t, d)
    p2 = p.reshape(depth, t, p.shape[-1])
    ret_w = RET_HEADS * RET_HEAD_DIM
    vec = lambda a: a.reshape(1, -1)
    v_first = None
    for i in range(depth):
        final = final_gain if i == depth - 1 else None
        w_gate, w_ple = _bf(ple_gate[i]), _bf(ple_proj[i])
        if i % 2 == 0:
            e = i // 2
            z = _even_in_proj(h, norm_gain[i], _bf(even_w_in[e]))
            mixed_a = _retention(z, batch, seq, ret_gn_gain[e])
            mixed_b = _stick_breaking(z, batch, seq)
            w_out = _bf(even_w_out[e])
            h = _out_call(_even_out_kernel, "even_out_proj_ple", h, [mixed_a, mixed_b],
                          [w_out[:ret_w], w_out[ret_w:]], p2[i], w_gate, w_ple, final)
        else:
            o = i // 2
            vres = None if v_first is None else (rwkv_v0[o - 1], _bf(rwkv_v1[o - 1]), _bf(rwkv_v2[o - 1]))
            outs = _odd_in_proj(
                h, seq, norm_gain[i], odd_mu[o], _bf(odd_w_in[o]), rwkv_w0[o], _bf(rwkv_w1[o]), _bf(rwkv_w2[o]),
                rwkv_a0[o], _bf(rwkv_a1[o]), _bf(rwkv_a2[o]), rwkv_k_k[o], rwkv_k_a[o], rwkv_r_k[o], vres, v_first)
            rd, kkd, kinv, binv, ktail, btail, vb, wc, bonus, gs = outs[:10]
            if v_first is None:
                v_first = outs[10]
            y = _rwkv_scan(rd, kkd, kinv, binv, ktail, btail, vb, wc, batch, seq)
            h = _out_call(_odd_out_kernel, "odd_out_proj_ple", h, [y, bonus, gs],
                          [vec(rwkv_lnx_gain[o]), vec(rwkv_lnx_bias[o]), _bf(odd_w_out[o])], p2[i], w_gate, w_ple,
                          final)
    return h.reshape(batch, seq, d)
```

```python
import functools

import jax
import jax.numpy as jnp
from jax import lax
from jax.experimental import pallas as pl
from jax.experimental.pallas import tpu as pltpu

F32 = jnp.float32
BF16 = jnp.bfloat16

LANES = 128
SUBLANES = 8
MXU_DIM = 256
VMEM_LIMIT_BYTES = 56 * 1024 * 1024

RET_HEADS = 4
RET_HEAD_DIM = 128
SB_HEADS = 4
SB_HEAD_DIM = 128
ROPE_BASE = 10000.0
RWKV_HEAD_DIM = 64
RMS_EPS = 1e-6
GN_EPS = 1e-5
LNX_EPS = 64e-5
F32_EXP_UNDERFLOW_LOG = -110.0

ROW_TILE = 512
ODD_ROW_TILE = 256
IN_COL_TILE = 1024
RET_CHUNK = 256
SB_TILE = 512
SCAN_CHUNK = 64
SCAN_CHUNKS_PER_STEP = 4
HEADS_PER_GROUP = MXU_DIM // RWKV_HEAD_DIM


def _bf(x):
    return x.astype(BF16)


def _dot(a, b):
    return jnp.dot(a, b, preferred_element_type=F32)


def _dot_nt(a, b):
    return lax.dot_general(a, b, (((1,), (1,)), ((), ())), preferred_element_type=F32)


def _dot_tn(a, b):
    return lax.dot_general(a, b, (((0,), (0,)), ((), ())), preferred_element_type=F32)


def _split(x):
    hi = _bf(x)
    lo = _bf(x - hi.astype(F32))
    return hi, lo


def _rms(x, gain):
    return x * lax.rsqrt(jnp.mean(x * x, axis=-1, keepdims=True) + RMS_EPS) * gain


def _softplus(y):
    return jnp.maximum(y, 0.0) + jnp.log(1.0 + jnp.exp(-jnp.abs(y)))


def _silu(g):
    return g * jax.nn.sigmoid(g)


def _head_ones():
    r = lax.broadcasted_iota(jnp.int32, (MXU_DIM, MXU_DIM), 0)
    c = lax.broadcasted_iota(jnp.int32, (MXU_DIM, MXU_DIM), 1)
    return jnp.where((r // RWKV_HEAD_DIM) == (c // RWKV_HEAD_DIM), 1.0, 0.0).astype(BF16)


def _headsum(x, ones_bd):
    hi, lo = _split(x)
    cols = []
    for s in range(x.shape[1] // MXU_DIM):
        sl = slice(s * MXU_DIM, (s + 1) * MXU_DIM)
        cols.append(_dot(hi[:, sl], ones_bd) + _dot(lo[:, sl], ones_bd))
    return jnp.concatenate(cols, axis=1)


def _params(*sem):
    return pltpu.CompilerParams(dimension_semantics=sem, vmem_limit_bytes=VMEM_LIMIT_BYTES)


def _even_in_kernel(h_ref, g_ref, w_ref, z_ref, hn_ref):
    @pl.when(pl.program_id(1) == 0)
    def _():
        hn_ref[...] = _bf(_rms(h_ref[...], g_ref[...]))

    z_ref[...] = _dot(hn_ref[...], w_ref[...])


def _even_in_proj(h, gain, w_in):
    t, d = h.shape
    f = w_in.shape[1]
    tm, tn = min(ROW_TILE, t), IN_COL_TILE
    return pl.pallas_call(
        _even_in_kernel,
        out_shape=jax.ShapeDtypeStruct((t, f), F32),
        grid=(t // tm, f // tn),
        in_specs=[
            pl.BlockSpec((tm, d), lambda i, j: (i, 0)),
            pl.BlockSpec((1, d), lambda i, j: (0, 0)),
            pl.BlockSpec((d, tn), lambda i, j: (0, j)),
        ],
        out_specs=pl.BlockSpec((tm, tn), lambda i, j: (i, j)),
        scratch_shapes=[pltpu.VMEM((tm, d), BF16)],
        compiler_params=_params("parallel", "arbitrary"),
        name="even_in_proj",
    )(h, gain.reshape(1, d), w_in)


def _ret_kernel(q_ref, k_ref, v_ref, g_ref, cos_ref, sin_ref, intra_ref, qdec_ref, kdec_ref, cdec_ref,
                gn_ref, o_ref, state_ref):
    @pl.when(pl.program_id(1) == 0)
    def _():
        state_ref[...] = jnp.zeros_like(state_ref)

    hd = RET_HEAD_DIM
    half = hd // 2
    cos = cos_ref[...]
    sin = sin_ref[...]
    hs = range(RET_HEADS)
    sl = [slice(h * hd, (h + 1) * hd) for h in hs]
    q = [q_ref[:, s] for s in sl]
    k = [k_ref[:, s] for s in sl]
    qr = [q[h] * cos + pltpu.roll(q[h], half, 1) * sin for h in hs]
    kr = [(k[h] * cos + pltpu.roll(k[h], half, 1) * sin) * (hd ** -0.5) for h in hs]
    v = [_bf(v_ref[:, s]) for s in sl]
    scores = [_dot_nt(_bf(qr[h]), _bf(kr[h])) * intra_ref[h] for h in hs]
    state = [state_ref[h] for h in hs]
    out = [_dot(_bf(scores[h]), v[h]) + _dot(_bf(qr[h] * qdec_ref[h]), _bf(state[h])) for h in hs]
    for h in hs:
        state_ref[h] = state[h] * cdec_ref[h, 0:1, :] + _dot_tn(_bf(kr[h] * kdec_ref[h]), v[h])
    for h in hs:
        mu = jnp.mean(out[h], axis=-1, keepdims=True)
        dev = out[h] - mu
        var = jnp.mean(dev * dev, axis=-1, keepdims=True)
        y = dev * lax.rsqrt(var + GN_EPS) * gn_ref[:, sl[h]]
        o_ref[:, sl[h]] = _bf(y * _silu(g_ref[:, sl[h]]))


def _retention(z, batch, seq, gn_gain):
    t = z.shape[0]
    c = min(RET_CHUNK, seq)
    nc = seq // c
    hd = RET_HEAD_DIM
    half = hd // 2
    inv_freq = ROPE_BASE ** (-jnp.arange(half, dtype=F32) / half)
    ang = jnp.arange(seq, dtype=F32)[:, None] * inv_freq[None, :]
    cos2 = jnp.concatenate([jnp.cos(ang), jnp.cos(ang)], axis=-1)
    sin2 = jnp.concatenate([-jnp.sin(ang), jnp.sin(ang)], axis=-1)
    lg = jnp.log(1.0 - 2.0 ** (-5.0 - jnp.arange(RET_HEADS, dtype=F32)))
    ci = jnp.arange(c, dtype=F32)
    diff = ci[:, None] - ci[None, :]
    intra = jnp.where(diff[None] >= 0, jnp.exp(jnp.maximum(diff, 0.0)[None] * lg[:, None, None]), 0.0)
    qdec = jnp.broadcast_to(jnp.exp((ci + 1)[None, :, None] * lg[:, None, None]), (RET_HEADS, c, hd))
    kdec = jnp.broadcast_to(jnp.exp((c - 1 - ci)[None, :, None] * lg[:, None, None]), (RET_HEADS, c, hd))
    cdec = jnp.broadcast_to(jnp.exp(c * lg)[:, None, None], (RET_HEADS, SUBLANES, hd))

    width = RET_HEADS * hd

    def zcol(j):
        return pl.BlockSpec((c, width), lambda b, n: (b * nc + n, j))

    full = lambda a: pl.BlockSpec(a.shape, lambda b, n: (0,) * a.ndim)
    gn = gn_gain.reshape(1, -1)
    return pl.pallas_call(
        _ret_kernel,
        out_shape=jax.ShapeDtypeStruct((t, width), BF16),
        grid=(batch, nc),
        in_specs=[
            zcol(0), zcol(1), zcol(2), zcol(3),
            pl.BlockSpec((c, hd), lambda b, n: (n, 0)),
            pl.BlockSpec((c, hd), lambda b, n: (n, 0)),
            full(intra), full(qdec), full(kdec), full(cdec), full(gn),
        ],
        out_specs=pl.BlockSpec((c, width), lambda b, n: (b * nc + n, 0)),
        scratch_shapes=[pltpu.VMEM((RET_HEADS, hd, hd), F32)],
        compiler_params=_params("parallel", "arbitrary"),
        name="retention",
    )(z, z, z, z, cos2, sin2, intra, qdec, kdec, cdec, gn)


def _sb_kernel(q_ref, k_ref, v_ref, g_ref, o_ref, acc_ref, run_ref, *, tile):
    i = pl.program_id(2)
    scale = SB_HEAD_DIM ** -0.5
    q = _bf(q_ref[...])
    grp = MXU_DIM
    ngrp = tile // grp
    trow = lax.broadcasted_iota(jnp.int32, (grp, grp), 0)
    tcol = lax.broadcasted_iota(jnp.int32, (grp, grp), 1)
    tri = jnp.where(trow >= tcol, 1.0, 0.0).astype(BF16)

    acc_ref[...] = jnp.zeros_like(acc_ref)
    run_ref[...] = jnp.zeros_like(run_ref)

    def span(kb, diagonal):
        ks = pl.multiple_of(kb * tile, tile)
        kblk = _bf(k_ref[pl.ds(ks, tile), :])
        vblk = _bf(v_ref[pl.ds(ks, tile), :])
        z = _dot_nt(q, kblk) * scale
        l1mb = -_softplus(z)
        if diagonal:
            mask = (lax.broadcasted_iota(jnp.int32, (tile, tile), 1)
                    < lax.broadcasted_iota(jnp.int32, (tile, tile), 0))
            l1mb = jnp.where(mask, l1mb, 0.0)
        hi, lo = _split(l1mb)
        run = run_ref[...]
        ws = [None] * ngrp
        for gi in reversed(range(ngrp)):
            sl = slice(gi * grp, (gi + 1) * grp)
            cs = _dot(hi[:, sl], tri) + _dot(lo[:, sl], tri)
            tot = z[:, sl] + cs + jnp.concatenate([run] * (grp // LANES), axis=1)
            w = jnp.exp(tot)
            if diagonal:
                w = jnp.where(mask[:, sl], w, 0.0)
            ws[gi] = _bf(w)
            run = run + jnp.broadcast_to(cs[:, 0:1], run.shape)
        acc_ref[...] += _dot(jnp.concatenate(ws, axis=1), vblk)
        run_ref[...] = run

    def alive():
        return (jnp.max(run_ref[...]) >= F32_EXP_UNDERFLOW_LOG).astype(jnp.int32)

    span(i, True)

    def body(carry):
        j, _ = carry
        span(i - 1 - j, False)
        return j + 1, alive()

    lax.while_loop(lambda c: (c[0] < i) & (c[1] > 0), body, (jnp.int32(0), alive()))
    o_ref[...] = _bf(acc_ref[...] * _silu(g_ref[...]))


def _stick_breaking(z, batch, seq):
    t = z.shape[0]
    hd = SB_HEAD_DIM
    tile = min(SB_TILE, seq)
    nq = seq // tile
    base = 4 * RET_HEADS
    return pl.pallas_call(
        functools.partial(_sb_kernel, tile=tile),
        out_shape=jax.ShapeDtypeStruct((t, SB_HEADS * hd), BF16),
        grid=(batch, SB_HEADS, nq),
        in_specs=[
            pl.BlockSpec((tile, hd), lambda b, h, i: (b * nq + i, base + h)),
            pl.BlockSpec((seq, hd), lambda b, h, i: (b, base + SB_HEADS + h)),
            pl.BlockSpec((seq, hd), lambda b, h, i: (b, base + 2 * SB_HEADS + h)),
            pl.BlockSpec((tile, hd), lambda b, h, i: (b * nq + i, base + 3 * SB_HEADS + h)),
        ],
        out_specs=pl.BlockSpec((tile, hd), lambda b, h, i: (b * nq + i, h)),
        scratch_shapes=[pltpu.VMEM((tile, hd), F32), pltpu.VMEM((tile, LANES), F32)],
        compiler_params=_params("parallel", "parallel", "arbitrary"),
        name="stick_breaking",
    )(z, z, z, z)


def _ple_tail(h1, p_ref, wg_ref, wp_ref, rest, final):
    gate = jax.nn.sigmoid(_dot(_bf(h1), wg_ref[...]))
    h2 = h1 + gate * _dot(_bf(p_ref[...]), wp_ref[...])
    if final:
        fg_ref, o_ref = rest
        o_ref[...] = _rms(h2, fg_ref[...])
    else:
        (o_ref,) = rest
        o_ref[...] = h2


def _even_out_kernel(h_ref, ma_ref, mb_ref, wa_ref, wb_ref, p_ref, wg_ref, wp_ref, *rest, final):
    h1 = h_ref[...] + _dot(ma_ref[...], wa_ref[...]) + _dot(mb_ref[...], wb_ref[...])
    _ple_tail(h1, p_ref, wg_ref, wp_ref, rest, final)


def _odd_out_kernel(h_ref, y_ref, bonus_ref, gs_ref, lg_ref, lb_ref, w_ref, p_ref, wg_ref, wp_ref, *rest, final):
    ones_bd = _head_ones()
    inv_n = 1.0 / RWKV_HEAD_DIM
    y = y_ref[...]
    dev = y - _headsum(y, ones_bd) * inv_n
    var = _headsum(dev * dev, ones_bd) * inv_n
    yn = dev * lax.rsqrt(var + LNX_EPS) * lg_ref[...] + lb_ref[...]
    mixed = _bf((yn + bonus_ref[...]) * gs_ref[...])
    _ple_tail(h_ref[...] + _dot(mixed, w_ref[...]), p_ref, wg_ref, wp_ref, rest, final)


def _out_call(kern, name, h, row_args, full_args_mid, p, w_gate, w_ple, final_gain):
    t, d = h.shape
    tm = min(ROW_TILE, t)
    final = final_gain is not None
    row = lambda a: pl.BlockSpec((tm, a.shape[1]), lambda i: (i, 0))
    full = lambda a: pl.BlockSpec(a.shape, lambda i: (0,) * a.ndim)
    args = [h, *row_args, *full_args_mid, p, w_gate, w_ple]
    specs = [row(h)] + [row(a) for a in row_args] + [full(a) for a in full_args_mid] + [row(p), full(w_gate),
                                                                                        full(w_ple)]
    if final:
        args.append(final_gain.reshape(1, d))
        specs.append(full(args[-1]))
    return pl.pallas_call(
        functools.partial(kern, final=final),
        out_shape=jax.ShapeDtypeStruct((t, d), F32),
        grid=(t // tm,),
        in_specs=specs,
        out_specs=row(h),
        compiler_params=_params("parallel"),
        name=name,
    )(*args)


def _odd_in_kernel(*refs, vres, tiles_per_seq, chunk):
    (h_ref, hp_ref, gain_ref, mu_ref, win_ref, w0_ref, w1_ref, w2_ref, a0_ref, a1_ref, a2_ref,
     kk_ref, ka_ref, rk_ref) = refs[:14]
    if vres:
        v0_ref, v1_ref, v2_ref, vf_ref = refs[14:18]
        outs = refs[18:]
    else:
        outs = refs[14:]
    rd_ref, kkd_ref, kinv_ref, binv_ref, kt_ref, bt_ref, vb_ref, wc_ref, bonus_ref, gs_ref = outs[:10]

    gain = gain_ref[...]
    hn = _rms(h_ref[...], gain)
    hnp = _rms(hp_ref[...], gain)
    tm = hn.shape[0]
    first = (pl.program_id(0) % tiles_per_seq) == 0
    prev_row = jnp.where(first, 0.0, hnp[SUBLANES - 1:SUBLANES, :])
    rowid = lax.broadcasted_iota(jnp.int32, hn.shape, 0)
    shifted = jnp.where(rowid == 0, prev_row, pltpu.roll(hn, 1, 0))
    xx = shifted - hn
    mix = lambda p: hn + xx * mu_ref[p:p + 1, :]
    r = _dot(_bf(mix(0)), win_ref[0])
    k = _dot(_bf(mix(1)), win_ref[1])
    xv = _bf(mix(2))
    v = _dot(xv, win_ref[2])
    g = _dot(_bf(mix(3)), win_ref[3])
    xw = _bf(mix(4))
    xa = _bf(mix(5))
    w_log = -_softplus(-(w0_ref[...] + _dot(_bf(jnp.tanh(_dot(xw, w1_ref[...]))), w2_ref[...]))) - 0.5
    lw = -jnp.exp(w_log)
    a = jax.nn.sigmoid(a0_ref[...] + _dot(_bf(_dot(xa, a1_ref[...])), a2_ref[...]))
    if vres:
        mixv = jax.nn.sigmoid(v0_ref[...] + _dot(_bf(_dot(xv, v1_ref[...])), v2_ref[...]))
        v = v + (vf_ref[...] - v) * mixv
    else:
        outs[10][...] = v
    ones_bd = _head_ones()
    kkraw = k * kk_ref[...]
    kk = kkraw * lax.rsqrt(jnp.maximum(_headsum(kkraw * kkraw, ones_bd), 1e-24))
    kmod = k * (1.0 + (a - 1.0) * ka_ref[...])
    b = kk * a

    ri = lax.broadcasted_iota(jnp.int32, (tm, tm), 0)
    ci = lax.broadcasted_iota(jnp.int32, (tm, tm), 1)
    same = (ri // chunk) == (ci // chunk)
    sel = jnp.concatenate([jnp.where(same & (ri >= ci), 1.0, 0.0), jnp.where(same, 1.0, 0.0)], axis=0).astype(BF16)
    hi, lo = _split(lw)
    both = _dot(sel, hi) + _dot(sel, lo)
    cum = both[:tm]
    wc = jnp.exp(both[tm:])
    w_inv = jnp.exp(-cum)
    kinv = kmod * w_inv
    binv = b * w_inv
    rd_ref[...] = _bf(r * jnp.exp(cum))
    kkd_ref[...] = _bf(kk * jnp.exp(cum - lw))
    kinv_ref[...] = _bf(kinv)
    binv_ref[...] = _bf(binv)
    kt_ref[...] = _bf(kinv * wc)
    bt_ref[...] = _bf(-(binv * wc))
    vb_ref[...] = _bf(v)
    for c in range(tm // chunk):
        wc_ref[c] = wc[c * chunk:c * chunk + 1, :]
    bonus_ref[...] = _headsum(r * kmod * rk_ref[...], ones_bd) * v
    gs_ref[...] = _silu(g)


def _odd_in_proj(h, seq, gain, mu, w_in, w0, w1, w2, a0, a1, a2, k_k, k_a, r_k, vres=None, v_first=None):
    t, d = h.shape
    tm = min(ODD_ROW_TILE, seq)
    chunk = SCAN_CHUNK
    tiles_per_seq = seq // tm
    sub_per_tile = tm // SUBLANES
    row = pl.BlockSpec((tm, d), lambda i: (i, 0))
    full = lambda a: pl.BlockSpec(a.shape, lambda i: (0,) * a.ndim)
    vec = lambda a: a.reshape(1, -1)
    args = [h, h, vec(gain), mu, w_in, vec(w0), w1, w2, vec(a0), a1, a2, vec(k_k), vec(k_a), vec(r_k)]
    specs = [row, pl.BlockSpec((SUBLANES, d), lambda i: (jnp.maximum(i * sub_per_tile - 1, 0), 0))]
    specs += [full(a) for a in args[2:]]
    if vres is not None:
        v0, v1, v2 = vres
        extra = [vec(v0), v1, v2]
        args += extra + [v_first]
        specs += [full(a) for a in extra] + [row]
    act = lambda dt: jax.ShapeDtypeStruct((t, d), dt)
    out_shape = [act(BF16)] * 7 + [jax.ShapeDtypeStruct((t // chunk, 1, d), F32), act(F32), act(F32)]
    out_specs = [row] * 7 + [pl.BlockSpec((tm // chunk, 1, d), lambda i: (i, 0, 0)), row, row]
    if vres is None:
        out_shape.append(act(F32))
        out_specs.append(row)
    return pl.pallas_call(
        functools.partial(_odd_in_kernel, vres=vres is not None, tiles_per_seq=tiles_per_seq, chunk=chunk),
        out_shape=out_shape,
        grid=(t // tm,),
        in_specs=specs,
        out_specs=out_specs,
        compiler_params=_params("parallel"),
        name="odd_in_proj",
    )(*args)


def _scan_kernel(rd_ref, kkd_ref, kinv_ref, binv_ref, kt_ref, bt_ref, v_ref, wc_ref, y_ref, state_ref, *, groups,
                 chunks):
    @pl.when(pl.program_id(1) == 0)
    def _():
        state_ref[...] = jnp.zeros_like(state_ref)

    c = rd_ref.shape[0] // chunks
    w = MXU_DIM
    hd = RWKV_HEAD_DIM
    row = lax.broadcasted_iota(jnp.int32, (c, w), 0)
    lane = lax.broadcasted_iota(jnp.int32, (c, w), 1)
    j = lane & (hd - 1)
    strict, incl, eye = j < row, j <= row, j == row
    blk16 = (j >> 4) == (row >> 4)
    blk32 = (j >> 5) == (row >> 5)
    head_of_lane = lane // hd
    head_masks = [head_of_lane == hh for hh in range(HEADS_PER_GROUP)]
    brow = lax.broadcasted_iota(jnp.int32, (w, w), 0)
    bcol = lax.broadcasted_iota(jnp.int32, (w, w), 1)
    bd = (brow // hd) == (bcol // hd)
    zero = jnp.zeros((), BF16)

    def stack(x):
        return jnp.concatenate([jnp.where(m, x, zero) for m in head_masks], axis=0)

    def pdot(x, y):
        return _dot(_bf(x), stack(_bf(y)))

    cols = [slice(g * w, (g + 1) * w) for g in range(groups)]
    rows = [slice(ch * c, (ch + 1) * c) for ch in range(chunks)]
    ix = [(r, s) for r in rows for s in cols]
    ks = range(len(ix))
    rd = [rd_ref[r, s] for r, s in ix]
    kkd = [kkd_ref[r, s] for r, s in ix]
    v = [v_ref[r, s] for r, s in ix]
    lhs2 = [jnp.concatenate([kkd[k], rd[k]], axis=0) for k in ks]
    ak = [_dot_nt(lhs2[k], stack(kinv_ref[ix[k][0], ix[k][1]])) for k in ks]
    ab = [_dot_nt(lhs2[k], stack(binv_ref[ix[k][0], ix[k][1]])) for k in ks]
    a_kk = [_bf(jnp.where(strict, ak[k][:c], 0.0)) for k in ks]
    a_rk = [_bf(jnp.where(incl, ak[k][c:], 0.0)) for k in ks]
    n = [jnp.where(strict, ab[k][:c], 0.0) for k in ks]
    a_rb = [_bf(jnp.where(incl, ab[k][c:], 0.0)) for k in ks]

    nd = [jnp.where(blk16, n[k], 0.0) for k in ks]
    n2 = [pdot(nd[k], nd[k]) for k in ks]
    n4 = [pdot(n2[k], n2[k]) for k in ks]
    n8 = [pdot(n4[k], n4[k]) for k in ks]
    tinv = [jnp.where(eye, 1.0, 0.0) - nd[k] for k in ks]
    tinv = [tinv[k] + pdot(tinv[k], n2[k]) for k in ks]
    tinv = [tinv[k] + pdot(tinv[k], n4[k]) for k in ks]
    tinv = [tinv[k] + pdot(tinv[k], n8[k]) for k in ks]
    off1 = [jnp.where(blk32 & jnp.logical_not(blk16), n[k], 0.0) for k in ks]
    m1 = [pdot(off1[k], tinv[k]) for k in ks]
    tinv = [tinv[k] - pdot(tinv[k], m1[k]) for k in ks]
    off2 = [jnp.where(blk32, 0.0, n[k]) for k in ks]
    m2 = [pdot(off2[k], tinv[k]) for k in ks]
    tinv = [_bf(tinv[k] - pdot(tinv[k], m2[k])) for k in ks]
    vst = [stack(v[k]) for k in ks]

    gs = range(groups)
    state = [state_ref[g] for g in gs]
    for ch in range(chunks):
        chain = [ch * groups + g for g in gs]
        sb = [_bf(state[g]) for g in gs]
        gmat = [_dot_nt(kkd[k], sb[g]) + _dot(a_kk[k], vst[k]) for g, k in zip(gs, chain)]
        skb = [_bf(_dot(tinv[k], stack(_bf(gmat[g])))) for g, k in zip(gs, chain)]
        for g, k in zip(gs, chain):
            r, s = ix[k]
            y_ref[r, s] = _dot_nt(rd[k], sb[g]) + _dot(a_rk[k], vst[k]) - _dot(a_rb[k], stack(skb[g]))
        upd = [_dot_tn(jnp.concatenate([v[k], skb[g]], axis=0),
                       jnp.concatenate([kt_ref[ix[k][0], ix[k][1]], bt_ref[ix[k][0], ix[k][1]]], axis=0))
               for g, k in zip(gs, chain)]
        state = [state[g] * wc_ref[ch, :, cols[g]] + jnp.where(bd, upd[g], 0.0) for g in gs]
    for g in gs:
        state_ref[g] = state[g]


def _rwkv_scan(rd, kkd, kinv, binv, ktail, btail, vb, wc, batch, seq):
    t, d = rd.shape
    chunks = SCAN_CHUNKS_PER_STEP
    rows = SCAN_CHUNK * chunks
    nc = seq // rows
    groups = d // MXU_DIM
    blk = pl.BlockSpec((rows, d), lambda b, n: (b * nc + n, 0))
    return pl.pallas_call(
        functools.partial(_scan_kernel, groups=groups, chunks=chunks),
        out_shape=jax.ShapeDtypeStruct((t, d), F32),
        grid=(batch, nc),
        in_specs=[blk] * 7 + [pl.BlockSpec((chunks, 1, d), lambda b, n: (b * nc + n, 0, 0))],
        out_specs=blk,
        scratch_shapes=[pltpu.VMEM((groups, MXU_DIM, MXU_DIM), F32)],
        compiler_params=_params("parallel", "arbitrary"),
        name="rwkv7_scan",
    )(rd, kkd, kinv, binv, ktail, btail, vb, wc)


def kernel(x, p, norm_gain, final_gain, ple_proj, ple_gate, even_w_in, even_w_out, ret_gn_gain, odd_mu, odd_w_in, odd_w_out, rwkv_w0, rwkv_w1, rwkv_w2, rwkv_a0, rwkv_a1, rwkv_a2, rwkv_v0, rwkv_v1, rwkv_v2, rwkv_k_k, rwkv_k_a, rwkv_r_k, rwkv_lnx_gain, rwkv_lnx_bias):
    batch, seq, d = x.shape
    depth = p.shape[0]
    t = batch * seq
    h = x.reshape(t, d)
    p2 = p.reshape(depth, t, p.shape[-1])
    ret_w = RET_HEADS * RET_HEAD_DIM
    vec = lambda a: a.reshape(1, -1)
    v_first = None
    for i in range(depth):
        final = final_gain if i == depth - 1 else None
        w_gate, w_ple = _bf(ple_gate[i]), _bf(ple_proj[i])
        if i % 2 == 0:
            e = i // 2
            z = _even_in_proj(h, norm_gain[i], _bf(even_w_in[e]))
            mixed_a = _retention(z, batch, seq, ret_gn_gain[e])
            mixed_b = _stick_breaking(z, batch, seq)
            w_out = _bf(even_w_out[e])
            h = _out_call(_even_out_kernel, "even_out_proj_ple", h, [mixed_a, mixed_b],
                          [w_out[:ret_w], w_out[ret_w:]], p2[i], w_gate, w_ple, final)
        else:
            o = i // 2
            vres = None if v_first is None else (rwkv_v0[o - 1], _bf(rwkv_v1[o - 1]), _bf(rwkv_v2[o - 1]))
            outs = _odd_in_proj(
                h, seq, norm_gain[i], odd_mu[o], _bf(odd_w_in[o]), rwkv_w0[o], _bf(rwkv_w1[o]), _bf(rwkv_w2[o]),
                rwkv_a0[o], _bf(rwkv_a1[o]), _bf(rwkv_a2[o]), rwkv_k_k[o], rwkv_k_a[o], rwkv_r_k[o], vres, v_first)
            rd, kkd, kinv, binv, ktail, btail, vb, wc, bonus, gs = outs[:10]
            if v_first is None:
                v_first = outs[10]
            y = _rwkv_scan(rd, kkd, kinv, binv, ktail, btail, vb, wc, batch, seq)
            h = _out_call(_odd_out_kernel, "odd_out_proj_ple", h, [y, bonus, gs],
                          [vec(rwkv_lnx_gain[o]), vec(rwkv_lnx_bias[o]), _bf(odd_w_out[o])], p2[i], w_gate, w_ple,
                          final)
    return h.reshape(batch, seq, d)
```

```python
import functools

import jax
import jax.numpy as jnp
from jax import lax
from jax.experimental import pallas as pl
from jax.experimental.pallas import tpu as pltpu

F32 = jnp.float32
BF16 = jnp.bfloat16

LANES = 128
SUBLANES = 8
MXU_DIM = 256
VMEM_LIMIT_BYTES = 56 * 1024 * 1024

RET_HEADS = 4
RET_HEAD_DIM = 128
SB_HEADS = 4
SB_HEAD_DIM = 128
ROPE_BASE = 10000.0
RWKV_HEAD_DIM = 64
RMS_EPS = 1e-6
GN_EPS = 1e-5
LNX_EPS = 64e-5
F32_EXP_UNDERFLOW_LOG = -110.0

ROW_TILE = 512
ODD_ROW_TILE = 256
IN_COL_TILE = 1024
RET_CHUNK = 256
SB_TILE = 256
SCAN_CHUNK = 64
SCAN_CHUNKS_PER_STEP = 4
HEADS_PER_GROUP = MXU_DIM // RWKV_HEAD_DIM


def _bf(x):
    return x.astype(BF16)


def _dot(a, b):
    return jnp.dot(a, b, preferred_element_type=F32)


def _dot_nt(a, b):
    return lax.dot_general(a, b, (((1,), (1,)), ((), ())), preferred_element_type=F32)


def _dot_tn(a, b):
    return lax.dot_general(a, b, (((0,), (0,)), ((), ())), preferred_element_type=F32)


def _split(x):
    hi = _bf(x)
    lo = _bf(x - hi.astype(F32))
    return hi, lo


def _rms(x, gain):
    return x * lax.rsqrt(jnp.mean(x * x, axis=-1, keepdims=True) + RMS_EPS) * gain


def _softplus(y):
    return jnp.maximum(y, 0.0) + jnp.log(1.0 + jnp.exp(-jnp.abs(y)))


def _silu(g):
    return g * jax.nn.sigmoid(g)


def _head_ones():
    r = lax.broadcasted_iota(jnp.int32, (MXU_DIM, MXU_DIM), 0)
    c = lax.broadcasted_iota(jnp.int32, (MXU_DIM, MXU_DIM), 1)
    return jnp.where((r // RWKV_HEAD_DIM) == (c // RWKV_HEAD_DIM), 1.0, 0.0).astype(BF16)


def _headsum(x, ones_bd):
    hi, lo = _split(x)
    cols = []
    for s in range(x.shape[1] // MXU_DIM):
        sl = slice(s * MXU_DIM, (s + 1) * MXU_DIM)
        cols.append(_dot(hi[:, sl], ones_bd) + _dot(lo[:, sl], ones_bd))
    return jnp.concatenate(cols, axis=1)


def _params(*sem):
    return pltpu.CompilerParams(dimension_semantics=sem, vmem_limit_bytes=VMEM_LIMIT_BYTES)


def _even_in_kernel(h_ref, g_ref, w_ref, z_ref, hn_ref):
    @pl.when(pl.program_id(1) == 0)
    def _():
        hn_ref[...] = _bf(_rms(h_ref[...], g_ref[...]))

    z_ref[...] = _bf(_dot(hn_ref[...], w_ref[...]))


def _even_in_proj(h, gain, w_in):
    t, d = h.shape
    f = w_in.shape[1]
    tm, tn = min(ROW_TILE, t), IN_COL_TILE
    return pl.pallas_call(
        _even_in_kernel,
        out_shape=jax.ShapeDtypeStruct((t, f), BF16),
        grid=(t // tm, f // tn),
        in_specs=[
            pl.BlockSpec((tm, d), lambda i, j: (i, 0)),
            pl.BlockSpec((1, d), lambda i, j: (0, 0)),
            pl.BlockSpec((d, tn), lambda i, j: (0, j)),
        ],
        out_specs=pl.BlockSpec((tm, tn), lambda i, j: (i, j)),
        scratch_shapes=[pltpu.VMEM((tm, d), BF16)],
        compiler_params=_params("parallel", "arbitrary"),
        name="even_in_proj",
    )(h, gain.reshape(1, d), w_in)


def _ret_kernel(q_ref, k_ref, v_ref, g_ref, cos_ref, sin_ref, intra_ref, qdec_ref, kdec_ref, cdec_ref,
                gn_ref, o_ref, state_ref):
    @pl.when(pl.program_id(1) == 0)
    def _():
        state_ref[...] = jnp.zeros_like(state_ref)

    hd = RET_HEAD_DIM
    half = hd // 2
    cos = cos_ref[...]
    sin = sin_ref[...]
    hs = range(RET_HEADS)
    sl = [slice(h * hd, (h + 1) * hd) for h in hs]
    q = [q_ref[:, s].astype(F32) for s in sl]
    k = [k_ref[:, s].astype(F32) for s in sl]
    qr = [q[h] * cos + pltpu.roll(q[h], half, 1) * sin for h in hs]
    kr = [(k[h] * cos + pltpu.roll(k[h], half, 1) * sin) * (hd ** -0.5) for h in hs]
    v = [v_ref[:, s] for s in sl]
    scores = [_dot_nt(_bf(qr[h]), _bf(kr[h])) * intra_ref[h] for h in hs]
    state = [state_ref[h] for h in hs]
    out = [_dot(_bf(scores[h]), v[h]) + _dot(_bf(qr[h] * qdec_ref[h]), _bf(state[h])) for h in hs]
    for h in hs:
        state_ref[h] = state[h] * cdec_ref[h, 0:1, :] + _dot_tn(_bf(kr[h] * kdec_ref[h]), v[h])
    for h in hs:
        mu = jnp.mean(out[h], axis=-1, keepdims=True)
        dev = out[h] - mu
        var = jnp.mean(dev * dev, axis=-1, keepdims=True)
        y = dev * lax.rsqrt(var + GN_EPS) * gn_ref[:, sl[h]]
        o_ref[:, sl[h]] = _bf(y * _silu(g_ref[:, sl[h]].astype(F32)))


def _retention(z, batch, seq, gn_gain):
    t = z.shape[0]
    c = min(RET_CHUNK, seq)
    nc = seq // c
    hd = RET_HEAD_DIM
    half = hd // 2
    inv_freq = ROPE_BASE ** (-jnp.arange(half, dtype=F32) / half)
    ang = jnp.arange(seq, dtype=F32)[:, None] * inv_freq[None, :]
    cos2 = jnp.concatenate([jnp.cos(ang), jnp.cos(ang)], axis=-1)
    sin2 = jnp.concatenate([-jnp.sin(ang), jnp.sin(ang)], axis=-1)
    lg = jnp.log(1.0 - 2.0 ** (-5.0 - jnp.arange(RET_HEADS, dtype=F32)))
    ci = jnp.arange(c, dtype=F32)
    diff = ci[:, None] - ci[None, :]
    intra = jnp.where(diff[None] >= 0, jnp.exp(jnp.maximum(diff, 0.0)[None] * lg[:, None, None]), 0.0)
    qdec = jnp.broadcast_to(jnp.exp((ci + 1)[None, :, None] * lg[:, None, None]), (RET_HEADS, c, hd))
    kdec = jnp.broadcast_to(jnp.exp((c - 1 - ci)[None, :, None] * lg[:, None, None]), (RET_HEADS, c, hd))
    cdec = jnp.broadcast_to(jnp.exp(c * lg)[:, None, None], (RET_HEADS, SUBLANES, hd))

    width = RET_HEADS * hd

    def zcol(j):
        return pl.BlockSpec((c, width), lambda b, n: (b * nc + n, j))

    full = lambda a: pl.BlockSpec(a.shape, lambda b, n: (0,) * a.ndim)
    gn = gn_gain.reshape(1, -1)
    return pl.pallas_call(
        _ret_kernel,
        out_shape=jax.ShapeDtypeStruct((t, width), BF16),
        grid=(batch, nc),
        in_specs=[
            zcol(0), zcol(1), zcol(2), zcol(3),
            pl.BlockSpec((c, hd), lambda b, n: (n, 0)),
            pl.BlockSpec((c, hd), lambda b, n: (n, 0)),
            full(intra), full(qdec), full(kdec), full(cdec), full(gn),
        ],
        out_specs=pl.BlockSpec((c, width), lambda b, n: (b * nc + n, 0)),
        scratch_shapes=[pltpu.VMEM((RET_HEADS, hd, hd), F32)],
        compiler_params=_params("parallel", "arbitrary"),
        name="retention",
    )(z, z, z, z, cos2, sin2, intra, qdec, kdec, cdec, gn)


def _sb_kernel(q_ref, k_ref, v_ref, g_ref, o_ref, acc_ref, run_ref, *, tile):
    i = pl.program_id(1)
    hd = SB_HEAD_DIM
    scale = hd ** -0.5
    hs = range(SB_HEADS)
    cols = [slice(h * hd, (h + 1) * hd) for h in hs]
    q = [q_ref[:, s] for s in cols]
    trow = lax.broadcasted_iota(jnp.int32, (tile, tile), 0)
    tcol = lax.broadcasted_iota(jnp.int32, (tile, tile), 1)
    tri = jnp.where(trow >= tcol, 1.0, 0.0).astype(BF16)
    causal = tcol < trow

    acc_ref[...] = jnp.zeros_like(acc_ref)
    run_ref[...] = jnp.zeros_like(run_ref)

    def span(kb, diagonal):
        ks = pl.multiple_of(kb * tile, tile)
        z = [_dot_nt(q[h], k_ref[pl.ds(ks, tile), cols[h]]) * scale for h in hs]
        l1mb = [-_softplus(z[h]) for h in hs]
        if diagonal:
            l1mb = [jnp.where(causal, l1mb[h], 0.0) for h in hs]
        parts = [_split(l1mb[h]) for h in hs]
        both = [_dot(jnp.concatenate(parts[h], axis=0), tri) for h in hs]
        cs = [both[h][:tile] + both[h][tile:] for h in hs]
        run = [run_ref[h] for h in hs]
        w = [jnp.exp(z[h] + cs[h] + jnp.concatenate([run[h]] * (tile // LANES), axis=1)) for h in hs]
        if diagonal:
            w = [jnp.where(causal, w[h], 0.0) for h in hs]
        for h in hs:
            acc_ref[h] += _dot(_bf(w[h]), v_ref[pl.ds(ks, tile), cols[h]])
            run_ref[h] = run[h] + jnp.broadcast_to(cs[h][:, 0:1], run[h].shape)

    def alive():
        return (jnp.max(run_ref[...]) >= F32_EXP_UNDERFLOW_LOG).astype(jnp.int32)

    span(i, True)

    def body(carry):
        j, _ = carry
        span(i - 1 - j, False)
        return j + 1, alive()

    lax.while_loop(lambda c: (c[0] < i) & (c[1] > 0), body, (jnp.int32(0), alive()))
    for h in hs:
        o_ref[:, cols[h]] = _bf(acc_ref[h] * _silu(g_ref[:, cols[h]].astype(F32)))


def _stick_breaking(z, batch, seq):
    t = z.shape[0]
    width = SB_HEADS * SB_HEAD_DIM
    tile = min(SB_TILE, seq)
    nq = seq // tile
    base = 4
    return pl.pallas_call(
        functools.partial(_sb_kernel, tile=tile),
        out_shape=jax.ShapeDtypeStruct((t, width), BF16),
        grid=(batch, nq),
        in_specs=[
            pl.BlockSpec((tile, width), lambda b, i: (b * nq + i, base)),
            pl.BlockSpec((seq, width), lambda b, i: (b, base + 1)),
            pl.BlockSpec((seq, width), lambda b, i: (b, base + 2)),
            pl.BlockSpec((tile, width), lambda b, i: (b * nq + i, base + 3)),
        ],
        out_specs=pl.BlockSpec((tile, width), lambda b, i: (b * nq + i, 0)),
        scratch_shapes=[pltpu.VMEM((SB_HEADS, tile, SB_HEAD_DIM), F32), pltpu.VMEM((SB_HEADS, tile, LANES), F32)],
        compiler_params=_params("parallel", "arbitrary"),
        name="stick_breaking",
    )(z, z, z, z)


def _ple_tail(h1, p_ref, wg_ref, wp_ref, rest, final):
    gate = jax.nn.sigmoid(_dot(_bf(h1), wg_ref[...]))
    h2 = h1 + gate * _dot(_bf(p_ref[...]), wp_ref[...])
    if final:
        fg_ref, o_ref = rest
        o_ref[...] = _rms(h2, fg_ref[...])
    else:
        (o_ref,) = rest
        o_ref[...] = h2


def _even_out_kernel(h_ref, ma_ref, mb_ref, wa_ref, wb_ref, p_ref, wg_ref, wp_ref, *rest, final):
    h1 = h_ref[...] + _dot(ma_ref[...], wa_ref[...]) + _dot(mb_ref[...], wb_ref[...])
    _ple_tail(h1, p_ref, wg_ref, wp_ref, rest, final)


def _odd_out_kernel(h_ref, y_ref, bonus_ref, gs_ref, lg_ref, lb_ref, w_ref, p_ref, wg_ref, wp_ref, *rest, final):
    ones_bd = _head_ones()
    inv_n = 1.0 / RWKV_HEAD_DIM
    y = y_ref[...]
    dev = y - _headsum(y, ones_bd) * inv_n
    var = _headsum(dev * dev, ones_bd) * inv_n
    yn = dev * lax.rsqrt(var + LNX_EPS) * lg_ref[...] + lb_ref[...]
    mixed = _bf((yn + bonus_ref[...]) * gs_ref[...])
    _ple_tail(h_ref[...] + _dot(mixed, w_ref[...]), p_ref, wg_ref, wp_ref, rest, final)


def _out_call(kern, name, h, row_args, full_args_mid, p, w_gate, w_ple, final_gain):
    t, d = h.shape
    tm = min(ROW_TILE, t)
    final = final_gain is not None
    row = lambda a: pl.BlockSpec((tm, a.shape[1]), lambda i: (i, 0))
    full = lambda a: pl.BlockSpec(a.shape, lambda i: (0,) * a.ndim)
    args = [h, *row_args, *full_args_mid, p, w_gate, w_ple]
    specs = [row(h)] + [row(a) for a in row_args] + [full(a) for a in full_args_mid] + [row(p), full(w_gate),
                                                                                        full(w_ple)]
    if final:
        args.append(final_gain.reshape(1, d))
        specs.append(full(args[-1]))
    return pl.pallas_call(
        functools.partial(kern, final=final),
        out_shape=jax.ShapeDtypeStruct((t, d), F32),
        grid=(t // tm,),
        in_specs=specs,
        out_specs=row(h),
        compiler_params=_params("parallel"),
        name=name,
    )(*args)


def _odd_in_kernel(*refs, vres, tiles_per_seq, chunk):
    (h_ref, hp_ref, gain_ref, mu_ref, win_ref, w0_ref, w1_ref, w2_ref, a0_ref, a1_ref, a2_ref,
     kk_ref, ka_ref, rk_ref) = refs[:14]
    if vres:
        v0_ref, v1_ref, v2_ref, vf_ref = refs[14:18]
        outs = refs[18:]
    else:
        outs = refs[14:]
    rd_ref, kkd_ref, kinv_ref, binv_ref, kt_ref, bt_ref, vb_ref, wc_ref, bonus_ref, gs_ref = outs[:10]

    gain = gain_ref[...]
    hn = _rms(h_ref[...], gain)
    hnp = _rms(hp_ref[...], gain)
    tm = hn.shape[0]
    first = (pl.program_id(0) % tiles_per_seq) == 0
    prev_row = jnp.where(first, 0.0, hnp[SUBLANES - 1:SUBLANES, :])
    rowid = lax.broadcasted_iota(jnp.int32, hn.shape, 0)
    shifted = jnp.where(rowid == 0, prev_row, pltpu.roll(hn, 1, 0))
    xx = shifted - hn
    mix = lambda p: hn + xx * mu_ref[p:p + 1, :]
    r = _dot(_bf(mix(0)), win_ref[0])
    k = _dot(_bf(mix(1)), win_ref[1])
    xv = _bf(mix(2))
    v = _dot(xv, win_ref[2])
    g = _dot(_bf(mix(3)), win_ref[3])
    xw = _bf(mix(4))
    xa = _bf(mix(5))
    w_log = -_softplus(-(w0_ref[...] + _dot(_bf(jnp.tanh(_dot(xw, w1_ref[...]))), w2_ref[...]))) - 0.5
    lw = -jnp.exp(w_log)
    a = jax.nn.sigmoid(a0_ref[...] + _dot(_bf(_dot(xa, a1_ref[...])), a2_ref[...]))
    if vres:
        mixv = jax.nn.sigmoid(v0_ref[...] + _dot(_bf(_dot(xv, v1_ref[...])), v2_ref[...]))
        v = v + (vf_ref[...] - v) * mixv
    else:
        outs[10][...] = v
    ones_bd = _head_ones()
    kkraw = k * kk_ref[...]
    kk = kkraw * lax.rsqrt(jnp.maximum(_headsum(kkraw * kkraw, ones_bd), 1e-24))
    kmod = k * (1.0 + (a - 1.0) * ka_ref[...])
    b = kk * a

    ri = lax.broadcasted_iota(jnp.int32, (tm, tm), 0)
    ci = lax.broadcasted_iota(jnp.int32, (tm, tm), 1)
    same = (ri // chunk) == (ci // chunk)
    sel = jnp.concatenate([jnp.where(same & (ri >= ci), 1.0, 0.0), jnp.where(same, 1.0, 0.0)], axis=0).astype(BF16)
    hi, lo = _split(lw)
    both = _dot(sel, hi) + _dot(sel, lo)
    cum = both[:tm]
    wc = jnp.exp(both[tm:])
    w_inv = jnp.exp(-cum)
    kinv = kmod * w_inv
    binv = b * w_inv
    rd_ref[...] = _bf(r * jnp.exp(cum))
    kkd_ref[...] = _bf(kk * jnp.exp(cum - lw))
    kinv_ref[...] = _bf(kinv)
    binv_ref[...] = _bf(binv)
    kt_ref[...] = _bf(kinv * wc)
    bt_ref[...] = _bf(-(binv * wc))
    vb_ref[...] = _bf(v)
    for c in range(tm // chunk):
        wc_ref[c] = wc[c * chunk:c * chunk + 1, :]
    bonus_ref[...] = _headsum(r * kmod * rk_ref[...], ones_bd) * v
    gs_ref[...] = _silu(g)


def _odd_in_proj(h, seq, gain, mu, w_in, w0, w1, w2, a0, a1, a2, k_k, k_a, r_k, vres=None, v_first=None):
    t, d = h.shape
    tm = min(ODD_ROW_TILE, seq)
    chunk = SCAN_CHUNK
    tiles_per_seq = seq // tm
    sub_per_tile = tm // SUBLANES
    row = pl.BlockSpec((tm, d), lambda i: (i, 0))
    full = lambda a: pl.BlockSpec(a.shape, lambda i: (0,) * a.ndim)
    vec = lambda a: a.reshape(1, -1)
    args = [h, h, vec(gain), mu, w_in, vec(w0), w1, w2, vec(a0), a1, a2, vec(k_k), vec(k_a), vec(r_k)]
    specs = [row, pl.BlockSpec((SUBLANES, d), lambda i: (jnp.maximum(i * sub_per_tile - 1, 0), 0))]
    specs += [full(a) for a in args[2:]]
    if vres is not None:
        v0, v1, v2 = vres
        extra = [vec(v0), v1, v2]
        args += extra + [v_first]
        specs += [full(a) for a in extra] + [row]
    act = lambda dt: jax.ShapeDtypeStruct((t, d), dt)
    out_shape = [act(BF16)] * 7 + [jax.ShapeDtypeStruct((t // chunk, 1, d), F32), act(F32), act(F32)]
    out_specs = [row] * 7 + [pl.BlockSpec((tm // chunk, 1, d), lambda i: (i, 0, 0)), row, row]
    if vres is None:
        out_shape.append(act(F32))
        out_specs.append(row)
    return pl.pallas_call(
        functools.partial(_odd_in_kernel, vres=vres is not None, tiles_per_seq=tiles_per_seq, chunk=chunk),
        out_shape=out_shape,
        grid=(t // tm,),
        in_specs=specs,
        out_specs=out_specs,
        compiler_params=_params("parallel"),
        name="odd_in_proj",
    )(*args)


def _scan_kernel(rd_ref, kkd_ref, kinv_ref, binv_ref, kt_ref, bt_ref, v_ref, wc_ref, y_ref, state_ref, *, groups,
                 chunks):
    @pl.when(pl.program_id(1) == 0)
    def _():
        state_ref[...] = jnp.zeros_like(state_ref)

    c = rd_ref.shape[0] // chunks
    w = MXU_DIM
    hd = RWKV_HEAD_DIM
    row = lax.broadcasted_iota(jnp.int32, (c, w), 0)
    lane = lax.broadcasted_iota(jnp.int32, (c, w), 1)
    j = lane & (hd - 1)
    strict, incl, eye = j < row, j <= row, j == row
    blk16 = (j >> 4) == (row >> 4)
    blk32 = (j >> 5) == (row >> 5)
    head_of_lane = lane // hd
    head_masks = [head_of_lane == hh for hh in range(HEADS_PER_GROUP)]
    brow = lax.broadcasted_iota(jnp.int32, (w, w), 0)
    bcol = lax.broadcasted_iota(jnp.int32, (w, w), 1)
    bd = (brow // hd) == (bcol // hd)
    zero = jnp.zeros((), BF16)

    def stack(x):
        return jnp.concatenate([jnp.where(m, x, zero) for m in head_masks], axis=0)

    def pdot(x, y):
        return _dot(_bf(x), stack(_bf(y)))

    cols = [slice(g * w, (g + 1) * w) for g in range(groups)]
    rows = [slice(ch * c, (ch + 1) * c) for ch in range(chunks)]
    ix = [(r, s) for r in rows for s in cols]
    ks = range(len(ix))
    rd = [rd_ref[r, s] for r, s in ix]
    kkd = [kkd_ref[r, s] for r, s in ix]
    v = [v_ref[r, s] for r, s in ix]
    lhs2 = [jnp.concatenate([kkd[k], rd[k]], axis=0) for k in ks]
    ak = [_dot_nt(lhs2[k], stack(kinv_ref[ix[k][0], ix[k][1]])) for k in ks]
    ab = [_dot_nt(lhs2[k], stack(binv_ref[ix[k][0], ix[k][1]])) for k in ks]
    a_kk = [_bf(jnp.where(strict, ak[k][:c], 0.0)) for k in ks]
    a_rk = [_bf(jnp.where(incl, ak[k][c:], 0.0)) for k in ks]
    n = [jnp.where(strict, ab[k][:c], 0.0) for k in ks]
    a_rb = [_bf(jnp.where(incl, ab[k][c:], 0.0)) for k in ks]

    nd = [jnp.where(blk16, n[k], 0.0) for k in ks]
    n2 = [pdot(nd[k], nd[k]) for k in ks]
    n4 = [pdot(n2[k], n2[k]) for k in ks]
    n8 = [pdot(n4[k], n4[k]) for k in ks]
    tinv = [jnp.where(eye, 1.0, 0.0) - nd[k] for k in ks]
    tinv = [tinv[k] + pdot(tinv[k], n2[k]) for k in ks]
    tinv = [tinv[k] + pdot(tinv[k], n4[k]) for k in ks]
    tinv = [tinv[k] + pdot(tinv[k], n8[k]) for k in ks]
    off1 = [jnp.where(blk32 & jnp.logical_not(blk16), n[k], 0.0) for k in ks]
    m1 = [pdot(off1[k], tinv[k]) for k in ks]
    tinv = [tinv[k] - pdot(tinv[k], m1[k]) for k in ks]
    off2 = [jnp.where(blk32, 0.0, n[k]) for k in ks]
    m2 = [pdot(off2[k], tinv[k]) for k in ks]
    tinv = [_bf(tinv[k] - pdot(tinv[k], m2[k])) for k in ks]
    vst = [stack(v[k]) for k in ks]

    gs = range(groups)
    state = [state_ref[g] for g in gs]
    for ch in range(chunks):
        chain = [ch * groups + g for g in gs]
        sb = [_bf(state[g]) for g in gs]
        gmat = [_dot_nt(kkd[k], sb[g]) + _dot(a_kk[k], vst[k]) for g, k in zip(gs, chain)]
        skb = [_bf(_dot(tinv[k], stack(_bf(gmat[g])))) for g, k in zip(gs, chain)]
        for g, k in zip(gs, chain):
            r, s = ix[k]
            y_ref[r, s] = _dot_nt(rd[k], sb[g]) + _dot(a_rk[k], vst[k]) - _dot(a_rb[k], stack(skb[g]))
        upd = [_dot_tn(jnp.concatenate([v[k], skb[g]], axis=0),
                       jnp.concatenate([kt_ref[ix[k][0], ix[k][1]], bt_ref[ix[k][0], ix[k][1]]], axis=0))
               for g, k in zip(gs, chain)]
        state = [state[g] * wc_ref[ch, :, cols[g]] + jnp.where(bd, upd[g], 0.0) for g in gs]
    for g in gs:
        state_ref[g] = state[g]


def _rwkv_scan(rd, kkd, kinv, binv, ktail, btail, vb, wc, batch, seq):
    t, d = rd.shape
    chunks = SCAN_CHUNKS_PER_STEP
    rows = SCAN_CHUNK * chunks
    nc = seq // rows
    groups = d // MXU_DIM
    blk = pl.BlockSpec((rows, d), lambda b, n: (b * nc + n, 0))
    return pl.pallas_call(
        functools.partial(_scan_kernel, groups=groups, chunks=chunks),
        out_shape=jax.ShapeDtypeStruct((t, d), F32),
        grid=(batch, nc),
        in_specs=[blk] * 7 + [pl.BlockSpec((chunks, 1, d), lambda b, n: (b * nc + n, 0, 0))],
        out_specs=blk,
        scratch_shapes=[pltpu.VMEM((groups, MXU_DIM, MXU_DIM), F32)],
        compiler_params=_params("parallel", "arbitrary"),
        name="rwkv7_scan",
    )(rd, kkd, kinv, binv, ktail, btail, vb, wc)


def kernel(x, p, norm_gain, final_gain, ple_proj, ple_gate, even_w_in, even_w_out, ret_gn_gain, odd_mu, odd_w_in, odd_w_out, rwkv_w0, rwkv_w1, rwkv_w2, rwkv_a0, rwkv_a1, rwkv_a2, rwkv_v0, rwkv_v1, rwkv_v2, rwkv_k_k, rwkv_k_a, rwkv_r_k, rwkv_lnx_gain, rwkv_lnx_bias):
    batch, seq, d = x.shape
    depth = p.shape[0]
    t = batch * seq
    h = x.reshape(t, d)
    p2 = p.reshape(depth, t, p.shape[-1])
    ret_w = RET_HEADS * RET_HEAD_DIM
    vec = lambda a: a.reshape(1, -1)
    v_first = None
    for i in range(depth):
        final = final_gain if i == depth - 1 else None
        w_gate, w_ple = _bf(ple_gate[i]), _bf(ple_proj[i])
        if i % 2 == 0:
            e = i // 2
            z = _even_in_proj(h, norm_gain[i], _bf(even_w_in[e]))
            mixed_a = _retention(z, batch, seq, ret_gn_gain[e])
            mixed_b = _stick_breaking(z, batch, seq)
            w_out = _bf(even_w_out[e])
            h = _out_call(_even_out_kernel, "even_out_proj_ple", h, [mixed_a, mixed_b],
                          [w_out[:ret_w], w_out[ret_w:]], p2[i], w_gate, w_ple, final)
        else:
            o = i // 2
            vres = None if v_first is None else (rwkv_v0[o - 1], _bf(rwkv_v1[o - 1]), _bf(rwkv_v2[o - 1]))
            outs = _odd_in_proj(
                h, seq, norm_gain[i], odd_mu[o], _bf(odd_w_in[o]), rwkv_w0[o], _bf(rwkv_w1[o]), _bf(rwkv_w2[o]),
                rwkv_a0[o], _bf(rwkv_a1[o]), _bf(rwkv_a2[o]), rwkv_k_k[o], rwkv_k_a[o], rwkv_r_k[o], vres, v_first)
            rd, kkd, kinv, binv, ktail, btail, vb, wc, bonus, gs = outs[:10]
            if v_first is None:
                v_first = outs[10]
            y = _rwkv_scan(rd, kkd, kinv, binv, ktail, btail, vb, wc, batch, seq)
            h = _out_call(_odd_out_kernel, "odd_out_proj_ple", h, [y, bonus, gs],
                          [vec(rwkv_lnx_gain[o]), vec(rwkv_lnx_bias[o]), _bf(odd_w_out[o])], p2[i], w_gate, w_ple,
                          final)
    return h.reshape(batch, seq, d)
```

```python
import functools

import jax
import jax.numpy as jnp
from jax import lax
from jax.experimental import pallas as pl
from jax.experimental.pallas import tpu as pltpu

F32 = jnp.float32
BF16 = jnp.bfloat16

LANES = 128
SUBLANES = 8
MXU_DIM = 256
VMEM_LIMIT_BYTES = 56 * 1024 * 1024

RET_HEADS = 4
RET_HEAD_DIM = 128
SB_HEADS = 4
SB_HEAD_DIM = 128
ROPE_BASE = 10000.0
RWKV_HEAD_DIM = 64
RMS_EPS = 1e-6
GN_EPS = 1e-5
LNX_EPS = 64e-5
F32_EXP_UNDERFLOW_LOG = -110.0

ROW_TILE = 512
ODD_ROW_TILE = 512
ODD_SUB_ROWS = 256
IN_COL_TILE = 1024
RET_CHUNK = 256
SB_TILE = 256
SCAN_CHUNK = 64
SCAN_CHUNKS_PER_STEP = 4
HEADS_PER_GROUP = MXU_DIM // RWKV_HEAD_DIM


def _bf(x):
    return x.astype(BF16)


def _dot(a, b):
    return jnp.dot(a, b, preferred_element_type=F32)


def _dot_nt(a, b):
    return lax.dot_general(a, b, (((1,), (1,)), ((), ())), preferred_element_type=F32)


def _dot_tn(a, b):
    return lax.dot_general(a, b, (((0,), (0,)), ((), ())), preferred_element_type=F32)


def _split(x):
    hi = _bf(x)
    lo = _bf(x - hi.astype(F32))
    return hi, lo


def _rms(x, gain):
    return x * lax.rsqrt(jnp.mean(x * x, axis=-1, keepdims=True) + RMS_EPS) * gain


def _softplus(y):
    return jnp.maximum(y, 0.0) + jnp.log(1.0 + jnp.exp(-jnp.abs(y)))


def _silu(g):
    return g * jax.nn.sigmoid(g)


def _head_ones():
    r = lax.broadcasted_iota(jnp.int32, (MXU_DIM, MXU_DIM), 0)
    c = lax.broadcasted_iota(jnp.int32, (MXU_DIM, MXU_DIM), 1)
    return jnp.where((r // RWKV_HEAD_DIM) == (c // RWKV_HEAD_DIM), 1.0, 0.0).astype(BF16)


def _headsum(x, ones_bd):
    hi, lo = _split(x)
    cols = []
    for s in range(x.shape[1] // MXU_DIM):
        sl = slice(s * MXU_DIM, (s + 1) * MXU_DIM)
        cols.append(_dot(hi[:, sl], ones_bd) + _dot(lo[:, sl], ones_bd))
    return jnp.concatenate(cols, axis=1)


def _params(*sem):
    return pltpu.CompilerParams(dimension_semantics=sem, vmem_limit_bytes=VMEM_LIMIT_BYTES)


def _even_in_kernel(h_ref, g_ref, w_ref, z_ref):
    hn = _bf(_rms(h_ref[...], g_ref[...]))
    for j in range(w_ref.shape[1] // IN_COL_TILE):
        sl = slice(j * IN_COL_TILE, (j + 1) * IN_COL_TILE)
        z_ref[:, sl] = _bf(_dot(hn, w_ref[:, sl]))


def _even_in_proj(h, gain, w_in):
    t, d = h.shape
    f = w_in.shape[1]
    tm = min(ROW_TILE, t)
    return pl.pallas_call(
        _even_in_kernel,
        out_shape=jax.ShapeDtypeStruct((t, f), BF16),
        grid=(t // tm,),
        in_specs=[
            pl.BlockSpec((tm, d), lambda i: (i, 0)),
            pl.BlockSpec((1, d), lambda i: (0, 0)),
            pl.BlockSpec((d, f), lambda i: (0, 0)),
        ],
        out_specs=pl.BlockSpec((tm, f), lambda i: (i, 0)),
        compiler_params=_params("parallel"),
        name="even_in_proj",
    )(h, gain.reshape(1, d), w_in)


def _ret_kernel(q_ref, k_ref, v_ref, g_ref, cos_ref, sin_ref, intra_ref, qdec_ref, kdec_ref, cdec_ref,
                gn_ref, o_ref, state_ref):
    @pl.when(pl.program_id(1) == 0)
    def _():
        state_ref[...] = jnp.zeros_like(state_ref)

    hd = RET_HEAD_DIM
    half = hd // 2
    cos = cos_ref[...]
    sin = sin_ref[...]
    hs = range(RET_HEADS)
    sl = [slice(h * hd, (h + 1) * hd) for h in hs]
    q = [q_ref[:, s].astype(F32) for s in sl]
    k = [k_ref[:, s].astype(F32) for s in sl]
    qr = [q[h] * cos + pltpu.roll(q[h], half, 1) * sin for h in hs]
    kr = [(k[h] * cos + pltpu.roll(k[h], half, 1) * sin) * (hd ** -0.5) for h in hs]
    v = [v_ref[:, s] for s in sl]
    scores = [_dot_nt(_bf(qr[h]), _bf(kr[h])) * intra_ref[h] for h in hs]
    state = [state_ref[h] for h in hs]
    out = [_dot(_bf(scores[h]), v[h]) + _dot(_bf(qr[h] * qdec_ref[h]), _bf(state[h])) for h in hs]
    for h in hs:
        state_ref[h] = state[h] * cdec_ref[h, 0:1, :] + _dot_tn(_bf(kr[h] * kdec_ref[h]), v[h])
    for h in hs:
        mu = jnp.mean(out[h], axis=-1, keepdims=True)
        dev = out[h] - mu
        var = jnp.mean(dev * dev, axis=-1, keepdims=True)
        y = dev * lax.rsqrt(var + GN_EPS) * gn_ref[:, sl[h]]
        o_ref[:, sl[h]] = _bf(y * _silu(g_ref[:, sl[h]].astype(F32)))


def _retention(z, batch, seq, gn_gain):
    t = z.shape[0]
    c = min(RET_CHUNK, seq)
    nc = seq // c
    hd = RET_HEAD_DIM
    half = hd // 2
    inv_freq = ROPE_BASE ** (-jnp.arange(half, dtype=F32) / half)
    ang = jnp.arange(seq, dtype=F32)[:, None] * inv_freq[None, :]
    cos2 = jnp.concatenate([jnp.cos(ang), jnp.cos(ang)], axis=-1)
    sin2 = jnp.concatenate([-jnp.sin(ang), jnp.sin(ang)], axis=-1)
    lg = jnp.log(1.0 - 2.0 ** (-5.0 - jnp.arange(RET_HEADS, dtype=F32)))
    ci = jnp.arange(c, dtype=F32)
    diff = ci[:, None] - ci[None, :]
    intra = jnp.where(diff[None] >= 0, jnp.exp(jnp.maximum(diff, 0.0)[None] * lg[:, None, None]), 0.0)
    qdec = jnp.broadcast_to(jnp.exp((ci + 1)[None, :, None] * lg[:, None, None]), (RET_HEADS, c, hd))
    kdec = jnp.broadcast_to(jnp.exp((c - 1 - ci)[None, :, None] * lg[:, None, None]), (RET_HEADS, c, hd))
    cdec = jnp.broadcast_to(jnp.exp(c * lg)[:, None, None], (RET_HEADS, SUBLANES, hd))

    width = RET_HEADS * hd

    def zcol(j):
        return pl.BlockSpec((c, width), lambda b, n: (b * nc + n, j))

    full = lambda a: pl.BlockSpec(a.shape, lambda b, n: (0,) * a.ndim)
    gn = gn_gain.reshape(1, -1)
    return pl.pallas_call(
        _ret_kernel,
        out_shape=jax.ShapeDtypeStruct((t, width), BF16),
        grid=(batch, nc),
        in_specs=[
            zcol(0), zcol(1), zcol(2), zcol(3),
            pl.BlockSpec((c, hd), lambda b, n: (n, 0)),
            pl.BlockSpec((c, hd), lambda b, n: (n, 0)),
            full(intra), full(qdec), full(kdec), full(cdec), full(gn),
        ],
        out_specs=pl.BlockSpec((c, width), lambda b, n: (b * nc + n, 0)),
        scratch_shapes=[pltpu.VMEM((RET_HEADS, hd, hd), F32)],
        compiler_params=_params("parallel", "arbitrary"),
        name="retention",
    )(z, z, z, z, cos2, sin2, intra, qdec, kdec, cdec, gn)


def _sb_kernel(q_ref, k_ref, v_ref, g_ref, o_ref, acc_ref, run_ref, *, tile):
    i = pl.program_id(1)
    hd = SB_HEAD_DIM
    scale = hd ** -0.5
    hs = range(SB_HEADS)
    cols = [slice(h * hd, (h + 1) * hd) for h in hs]
    q = [q_ref[:, s] for s in cols]
    trow = lax.broadcasted_iota(jnp.int32, (tile, tile), 0)
    tcol = lax.broadcasted_iota(jnp.int32, (tile, tile), 1)
    tri = jnp.where(trow >= tcol, 1.0, 0.0).astype(BF16)
    causal = tcol < trow

    acc_ref[...] = jnp.zeros_like(acc_ref)
    run_ref[...] = jnp.zeros_like(run_ref)

    def span(kb, diagonal):
        ks = pl.multiple_of(kb * tile, tile)
        z = [_dot_nt(q[h], k_ref[pl.ds(ks, tile), cols[h]]) * scale for h in hs]
        l1mb = [-_softplus(z[h]) for h in hs]
        if diagonal:
            l1mb = [jnp.where(causal, l1mb[h], 0.0) for h in hs]
        parts = [_split(l1mb[h]) for h in hs]
        both = [_dot(jnp.concatenate(parts[h], axis=0), tri) for h in hs]
        cs = [both[h][:tile] + both[h][tile:] for h in hs]
        run = [run_ref[h] for h in hs]
        w = [jnp.exp(z[h] + cs[h] + jnp.concatenate([run[h]] * (tile // LANES), axis=1)) for h in hs]
        if diagonal:
            w = [jnp.where(causal, w[h], 0.0) for h in hs]
        for h in hs:
            acc_ref[h] += _dot(_bf(w[h]), v_ref[pl.ds(ks, tile), cols[h]])
            run_ref[h] = run[h] + jnp.broadcast_to(cs[h][:, 0:1], run[h].shape)

    def alive():
        return (jnp.max(run_ref[...]) >= F32_EXP_UNDERFLOW_LOG).astype(jnp.int32)

    span(i, True)

    def body(carry):
        j, _ = carry
        span(i - 1 - j, False)
        return j + 1, alive()

    lax.while_loop(lambda c: (c[0] < i) & (c[1] > 0), body, (jnp.int32(0), alive()))
    for h in hs:
        o_ref[:, cols[h]] = _bf(acc_ref[h] * _silu(g_ref[:, cols[h]].astype(F32)))


def _stick_breaking(z, batch, seq):
    t = z.shape[0]
    width = SB_HEADS * SB_HEAD_DIM
    tile = min(SB_TILE, seq)
    nq = seq // tile
    base = 4
    return pl.pallas_call(
        functools.partial(_sb_kernel, tile=tile),
        out_shape=jax.ShapeDtypeStruct((t, width), BF16),
        grid=(batch, nq),
        in_specs=[
            pl.BlockSpec((tile, width), lambda b, i: (b * nq + i, base)),
            pl.BlockSpec((seq, width), lambda b, i: (b, base + 1)),
            pl.BlockSpec((seq, width), lambda b, i: (b, base + 2)),
            pl.BlockSpec((tile, width), lambda b, i: (b * nq + i, base + 3)),
        ],
        out_specs=pl.BlockSpec((tile, width), lambda b, i: (b * nq + i, 0)),
        scratch_shapes=[pltpu.VMEM((SB_HEADS, tile, SB_HEAD_DIM), F32), pltpu.VMEM((SB_HEADS, tile, LANES), F32)],
        compiler_params=_params("parallel", "arbitrary"),
        name="stick_breaking",
    )(z, z, z, z)


def _ple_tail(h1, p_ref, wg_ref, wp_ref, rest, final):
    gate = jax.nn.sigmoid(_dot(_bf(h1), wg_ref[...]))
    h2 = h1 + gate * _dot(_bf(p_ref[...]), wp_ref[...])
    if final:
        fg_ref, o_ref = rest
        o_ref[...] = _rms(h2, fg_ref[...])
    else:
        (o_ref,) = rest
        o_ref[...] = h2


def _even_out_kernel(h_ref, ma_ref, mb_ref, wa_ref, wb_ref, p_ref, wg_ref, wp_ref, *rest, final):
    h1 = h_ref[...] + _dot(ma_ref[...], wa_ref[...]) + _dot(mb_ref[...], wb_ref[...])
    _ple_tail(h1, p_ref, wg_ref, wp_ref, rest, final)


def _odd_out_kernel(h_ref, y_ref, bonus_ref, gs_ref, lg_ref, lb_ref, w_ref, p_ref, wg_ref, wp_ref, *rest, final):
    ones_bd = _head_ones()
    inv_n = 1.0 / RWKV_HEAD_DIM
    y = y_ref[...]
    dev = y - _headsum(y, ones_bd) * inv_n
    var = _headsum(dev * dev, ones_bd) * inv_n
    yn = dev * lax.rsqrt(var + LNX_EPS) * lg_ref[...] + lb_ref[...]
    mixed = _bf((yn + bonus_ref[...]) * gs_ref[...])
    _ple_tail(h_ref[...] + _dot(mixed, w_ref[...]), p_ref, wg_ref, wp_ref, rest, final)


def _out_call(kern, name, h, row_args, full_args_mid, p, w_gate, w_ple, final_gain):
    t, d = h.shape
    tm = min(ROW_TILE, t)
    final = final_gain is not None
    row = lambda a: pl.BlockSpec((tm, a.shape[1]), lambda i: (i, 0))
    full = lambda a: pl.BlockSpec(a.shape, lambda i: (0,) * a.ndim)
    args = [h, *row_args, *full_args_mid, p, w_gate, w_ple]
    specs = [row(h)] + [row(a) for a in row_args] + [full(a) for a in full_args_mid] + [row(p), full(w_gate),
                                                                                        full(w_ple)]
    if final:
        args.append(final_gain.reshape(1, d))
        specs.append(full(args[-1]))
    return pl.pallas_call(
        functools.partial(kern, final=final),
        out_shape=jax.ShapeDtypeStruct((t, d), F32),
        grid=(t // tm,),
        in_specs=specs,
        out_specs=row(h),
        compiler_params=_params("parallel"),
        name=name,
    )(*args)


def _odd_in_kernel(*refs, vres, tiles_per_seq, chunk, sub):
    (h_ref, hp_ref, gain_ref, mu_ref, win_ref, w0_ref, w1_ref, w2_ref, a0_ref, a1_ref, a2_ref,
     kk_ref, ka_ref, rk_ref) = refs[:14]
    if vres:
        v0_ref, v1_ref, v2_ref, vf_ref = refs[14:18]
        outs = refs[18:]
    else:
        outs = refs[14:]
    rd_ref, kkd_ref, kinv_ref, binv_ref, kt_ref, bt_ref, vb_ref, wc_ref, bonus_ref, gs_ref = outs[:10]

    gain = gain_ref[...]
    hn_all = _rms(h_ref[...], gain)
    hnp = _rms(hp_ref[...], gain)
    tm = hn_all.shape[0]
    first = (pl.program_id(0) % tiles_per_seq) == 0
    prev_row = jnp.where(first, 0.0, hnp[SUBLANES - 1:SUBLANES, :])
    rowid = lax.broadcasted_iota(jnp.int32, hn_all.shape, 0)
    xx_all = jnp.where(rowid == 0, prev_row, pltpu.roll(hn_all, 1, 0)) - hn_all
    ones_bd = _head_ones()
    ri = lax.broadcasted_iota(jnp.int32, (sub, sub), 0)
    ci = lax.broadcasted_iota(jnp.int32, (sub, sub), 1)
    sel = jnp.where(((ri // chunk) == (ci // chunk)) & (ri >= ci), 1.0, 0.0).astype(BF16)

    def part(r0):
        rows = slice(r0, r0 + sub)
        hn, xx = hn_all[rows], xx_all[rows]
        mix = lambda p: hn + xx * mu_ref[p:p + 1, :]
        r = _dot(_bf(mix(0)), win_ref[0])
        yield
        k = _dot(_bf(mix(1)), win_ref[1])
        yield
        xv = _bf(mix(2))
        v = _dot(xv, win_ref[2])
        yield
        g = _dot(_bf(mix(3)), win_ref[3])
        yield
        xw = _bf(mix(4))
        xa = _bf(mix(5))
        w_log = -_softplus(-(w0_ref[...] + _dot(_bf(jnp.tanh(_dot(xw, w1_ref[...]))), w2_ref[...]))) - 0.5
        lw = -jnp.exp(w_log)
        a = jax.nn.sigmoid(a0_ref[...] + _dot(_bf(_dot(xa, a1_ref[...])), a2_ref[...]))
        if vres:
            mixv = jax.nn.sigmoid(v0_ref[...] + _dot(_bf(_dot(xv, v1_ref[...])), v2_ref[...]))
            v = v + (vf_ref[rows, :] - v) * mixv
        else:
            outs[10][rows, :] = v
        yield
        kkraw = k * kk_ref[...]
        kk = kkraw * lax.rsqrt(jnp.maximum(_headsum(kkraw * kkraw, ones_bd), 1e-24))
        kmod = k * (1.0 + (a - 1.0) * ka_ref[...])
        b = kk * a
        hi, lo = _split(lw)
        cum = _dot(sel, hi) + _dot(sel, lo)
        ends = [jnp.exp(cum[(c + 1) * chunk - 1:(c + 1) * chunk, :]) for c in range(sub // chunk)]
        wc = jnp.concatenate([jnp.broadcast_to(e, (chunk, e.shape[1])) for e in ends], axis=0)
        yield
        w_inv = jnp.exp(-cum)
        kinv = kmod * w_inv
        binv = b * w_inv
        rd_ref[rows, :] = _bf(r * jnp.exp(cum))
        kkd_ref[rows, :] = _bf(kk * jnp.exp(cum - lw))
        kinv_ref[rows, :] = _bf(kinv)
        binv_ref[rows, :] = _bf(binv)
        yield
        kt_ref[rows, :] = _bf(kinv * wc)
        bt_ref[rows, :] = _bf(-(binv * wc))
        vb_ref[rows, :] = _bf(v)
        for c in range(sub // chunk):
            wc_ref[r0 // chunk + c] = ends[c]
        bonus_ref[rows, :] = _headsum(r * kmod * rk_ref[...], ones_bd) * v
        gs_ref[rows, :] = _silu(g)
        yield

    parts = [part(r0) for r0 in range(0, tm, sub)]
    for _ in range(4):
        next(parts[0])
    for p in range(1, len(parts)):
        for _ in zip(parts[p - 1], parts[p]):
            pass
    for _ in parts[-1]:
        pass


def _odd_in_proj(h, seq, gain, mu, w_in, w0, w1, w2, a0, a1, a2, k_k, k_a, r_k, vres=None, v_first=None):
    t, d = h.shape
    tm = min(ODD_ROW_TILE, seq)
    chunk = SCAN_CHUNK
    tiles_per_seq = seq // tm
    sub_per_tile = tm // SUBLANES
    row = pl.BlockSpec((tm, d), lambda i: (i, 0))
    full = lambda a: pl.BlockSpec(a.shape, lambda i: (0,) * a.ndim)
    vec = lambda a: a.reshape(1, -1)
    args = [h, h, vec(gain), mu, w_in, vec(w0), w1, w2, vec(a0), a1, a2, vec(k_k), vec(k_a), vec(r_k)]
    specs = [row, pl.BlockSpec((SUBLANES, d), lambda i: (jnp.maximum(i * sub_per_tile - 1, 0), 0))]
    specs += [full(a) for a in args[2:]]
    if vres is not None:
        v0, v1, v2 = vres
        extra = [vec(v0), v1, v2]
        args += extra + [v_first]
        specs += [full(a) for a in extra] + [row]
    act = lambda dt: jax.ShapeDtypeStruct((t, d), dt)
    out_shape = [act(BF16)] * 7 + [jax.ShapeDtypeStruct((t // chunk, 1, d), F32), act(F32), act(F32)]
    out_specs = [row] * 7 + [pl.BlockSpec((tm // chunk, 1, d), lambda i: (i, 0, 0)), row, row]
    if vres is None:
        out_shape.append(act(F32))
        out_specs.append(row)
    return pl.pallas_call(
        functools.partial(_odd_in_kernel, vres=vres is not None, tiles_per_seq=tiles_per_seq, chunk=chunk,
                          sub=min(ODD_SUB_ROWS, tm)),
        out_shape=out_shape,
        grid=(t // tm,),
        in_specs=specs,
        out_specs=out_specs,
        compiler_params=_params("parallel"),
        name="odd_in_proj",
    )(*args)


def _scan_kernel(rd_ref, kkd_ref, kinv_ref, binv_ref, kt_ref, bt_ref, v_ref, wc_ref, y_ref, state_ref, *, groups,
                 chunks):
    @pl.when(pl.program_id(1) == 0)
    def _():
        state_ref[...] = jnp.zeros_like(state_ref)

    c = rd_ref.shape[0] // chunks
    w = MXU_DIM
    hd = RWKV_HEAD_DIM
    row = lax.broadcasted_iota(jnp.int32, (c, w), 0)
    lane = lax.broadcasted_iota(jnp.int32, (c, w), 1)
    j = lane & (hd - 1)
    strict, incl, eye = j < row, j <= row, j == row
    blk16 = (j >> 4) == (row >> 4)
    blk32 = (j >> 5) == (row >> 5)
    head_of_lane = lane // hd
    head_masks = [head_of_lane == hh for hh in range(HEADS_PER_GROUP)]
    brow = lax.broadcasted_iota(jnp.int32, (w, w), 0)
    bcol = lax.broadcasted_iota(jnp.int32, (w, w), 1)
    bd = (brow // hd) == (bcol // hd)
    zero = jnp.zeros((), BF16)

    def stack(x):
        return jnp.concatenate([jnp.where(m, x, zero) for m in head_masks], axis=0)

    def pdot(x, y):
        return _dot(_bf(x), stack(_bf(y)))

    cols = [slice(g * w, (g + 1) * w) for g in range(groups)]
    rows = [slice(ch * c, (ch + 1) * c) for ch in range(chunks)]
    ix = [(r, s) for r in rows for s in cols]
    ks = range(len(ix))
    rd = [rd_ref[r, s] for r, s in ix]
    kkd = [kkd_ref[r, s] for r, s in ix]
    v = [v_ref[r, s] for r, s in ix]
    lhs2 = [jnp.concatenate([kkd[k], rd[k]], axis=0) for k in ks]
    ak = [_dot_nt(lhs2[k], stack(kinv_ref[ix[k][0], ix[k][1]])) for k in ks]
    ab = [_dot_nt(lhs2[k], stack(binv_ref[ix[k][0], ix[k][1]])) for k in ks]
    a_kk = [_bf(jnp.where(strict, ak[k][:c], 0.0)) for k in ks]
    a_rk = [_bf(jnp.where(incl, ak[k][c:], 0.0)) for k in ks]
    n = [jnp.where(strict, ab[k][:c], 0.0) for k in ks]
    a_rb = [_bf(jnp.where(incl, ab[k][c:], 0.0)) for k in ks]

    nd = [jnp.where(blk16, n[k], 0.0) for k in ks]
    n2 = [pdot(nd[k], nd[k]) for k in ks]
    n4 = [pdot(n2[k], n2[k]) for k in ks]
    n8 = [pdot(n4[k], n4[k]) for k in ks]
    tinv = [jnp.where(eye, 1.0, 0.0) - nd[k] for k in ks]
    tinv = [tinv[k] + pdot(tinv[k], n2[k]) for k in ks]
    tinv = [tinv[k] + pdot(tinv[k], n4[k]) for k in ks]
    tinv = [tinv[k] + pdot(tinv[k], n8[k]) for k in ks]
    off1 = [jnp.where(blk32 & jnp.logical_not(blk16), n[k], 0.0) for k in ks]
    m1 = [pdot(off1[k], tinv[k]) for k in ks]
    tinv = [tinv[k] - pdot(tinv[k], m1[k]) for k in ks]
    off2 = [jnp.where(blk32, 0.0, n[k]) for k in ks]
    m2 = [pdot(off2[k], tinv[k]) for k in ks]
    tinv = [_bf(tinv[k] - pdot(tinv[k], m2[k])) for k in ks]
    vst = [stack(v[k]) for k in ks]

    gs = range(groups)
    state = [state_ref[g] for g in gs]
    for ch in range(chunks):
        chain = [ch * groups + g for g in gs]
        sb = [_bf(state[g]) for g in gs]
        gmat = [_dot_nt(kkd[k], sb[g]) + _dot(a_kk[k], vst[k]) for g, k in zip(gs, chain)]
        skb = [_bf(_dot(tinv[k], stack(_bf(gmat[g])))) for g, k in zip(gs, chain)]
        for g, k in zip(gs, chain):
            r, s = ix[k]
            y_ref[r, s] = _dot_nt(rd[k], sb[g]) + _dot(a_rk[k], vst[k]) - _dot(a_rb[k], stack(skb[g]))
        upd = [_dot_tn(jnp.concatenate([v[k], skb[g]], axis=0),
                       jnp.concatenate([kt_ref[ix[k][0], ix[k][1]], bt_ref[ix[k][0], ix[k][1]]], axis=0))
               for g, k in zip(gs, chain)]
        state = [state[g] * wc_ref[ch, :, cols[g]] + jnp.where(bd, upd[g], 0.0) for g in gs]
    for g in gs:
        state_ref[g] = state[g]


def _rwkv_scan(rd, kkd, kinv, binv, ktail, btail, vb, wc, batch, seq):
    t, d = rd.shape
    chunks = SCAN_CHUNKS_PER_STEP
    rows = SCAN_CHUNK * chunks
    nc = seq // rows
    groups = d // MXU_DIM
    blk = pl.BlockSpec((rows, d), lambda b, n: (b * nc + n, 0))
    return pl.pallas_call(
        functools.partial(_scan_kernel, groups=groups, chunks=chunks),
        out_shape=jax.ShapeDtypeStruct((t, d), F32),
        grid=(batch, nc),
        in_specs=[blk] * 7 + [pl.BlockSpec((chunks, 1, d), lambda b, n: (b * nc + n, 0, 0))],
        out_specs=blk,
        scratch_shapes=[pltpu.VMEM((groups, MXU_DIM, MXU_DIM), F32)],
        compiler_params=_params("parallel", "arbitrary"),
        name="rwkv7_scan",
    )(rd, kkd, kinv, binv, ktail, btail, vb, wc)


def kernel(x, p, norm_gain, final_gain, ple_proj, ple_gate, even_w_in, even_w_out, ret_gn_gain, odd_mu, odd_w_in, odd_w_out, rwkv_w0, rwkv_w1, rwkv_w2, rwkv_a0, rwkv_a1, rwkv_a2, rwkv_v0, rwkv_v1, rwkv_v2, rwkv_k_k, rwkv_k_a, rwkv_r_k, rwkv_lnx_gain, rwkv_lnx_bias):
    batch, seq, d = x.shape
    depth = p.shape[0]
    t = batch * seq
    h = x.reshape(t, d)
    p2 = p.reshape(depth, t, p.shape[-1])
    ret_w = RET_HEADS * RET_HEAD_DIM
    vec = lambda a: a.reshape(1, -1)
    v_first = None
    for i in range(depth):
        final = final_gain if i == depth - 1 else None
        w_gate, w_ple = _bf(ple_gate[i]), _bf(ple_proj[i])
        if i % 2 == 0:
            e = i // 2
            z = _even_in_proj(h, norm_gain[i], _bf(even_w_in[e]))
            mixed_a = _retention(z, batch, seq, ret_gn_gain[e])
            mixed_b = _stick_breaking(z, batch, seq)
            w_out = _bf(even_w_out[e])
            h = _out_call(_even_out_kernel, "even_out_proj_ple", h, [mixed_a, mixed_b],
                          [w_out[:ret_w], w_out[ret_w:]], p2[i], w_gate, w_ple, final)
        else:
            o = i // 2
            vres = None if v_first is None else (rwkv_v0[o - 1], _bf(rwkv_v1[o - 1]), _bf(rwkv_v2[o - 1]))
            outs = _odd_in_proj(
                h, seq, norm_gain[i], odd_mu[o], _bf(odd_w_in[o]), rwkv_w0[o], _bf(rwkv_w1[o]), _bf(rwkv_w2[o]),
                rwkv_a0[o], _bf(rwkv_a1[o]), _bf(rwkv_a2[o]), rwkv_k_k[o], rwkv_k_a[o], rwkv_r_k[o], vres, v_first)
            rd, kkd, kinv, binv, ktail, btail, vb, wc, bonus, gs = outs[:10]
            if v_first is None:
                v_first = outs[10]
            y = _rwkv_scan(rd, kkd, kinv, binv, ktail, btail, vb, wc, batch, seq)
            h = _out_call(_odd_out_kernel, "odd_out_proj_ple", h, [y, bonus, gs],
                          [vec(rwkv_lnx_gain[o]), vec(rwkv_lnx_bias[o]), _bf(odd_w_out[o])], p2[i], w_gate, w_ple,
                          final)
    return h.reshape(batch, seq, d)
```

```python
import functools

import jax
import jax.numpy as jnp
from jax import lax
from jax.experimental import pallas as pl
from jax.experimental.pallas import tpu as pltpu

F32 = jnp.float32
BF16 = jnp.bfloat16

LANES = 128
SUBLANES = 8
MXU_DIM = 256
VMEM_LIMIT_BYTES = 56 * 1024 * 1024

RET_HEADS = 4
RET_HEAD_DIM = 128
SB_HEADS = 4
SB_HEAD_DIM = 128
ROPE_BASE = 10000.0
RWKV_HEAD_DIM = 64
RMS_EPS = 1e-6
GN_EPS = 1e-5
LNX_EPS = 64e-5
F32_EXP_UNDERFLOW_LOG = -110.0

ROW_TILE = 512
ODD_ROW_TILE = 512
ODD_SUB_ROWS = 256
IN_COL_TILE = 1024
RET_CHUNK = 256
SB_TILE = 256
SCAN_CHUNK = 64
SCAN_CHUNKS_PER_STEP = 4
HEADS_PER_GROUP = MXU_DIM // RWKV_HEAD_DIM


def _bf(x):
    return x.astype(BF16)


def _dot(a, b):
    return jnp.dot(a, b, preferred_element_type=F32)


def _dot_nt(a, b):
    return lax.dot_general(a, b, (((1,), (1,)), ((), ())), preferred_element_type=F32)


def _dot_tn(a, b):
    return lax.dot_general(a, b, (((0,), (0,)), ((), ())), preferred_element_type=F32)


def _split(x):
    hi = _bf(x)
    lo = _bf(x - hi.astype(F32))
    return hi, lo


def _rms(x, gain):
    return x * lax.rsqrt(jnp.mean(x * x, axis=-1, keepdims=True) + RMS_EPS) * gain


def _softplus(y):
    return jnp.maximum(y, 0.0) + jnp.log(1.0 + jnp.exp(-jnp.abs(y)))


def _silu(g):
    return g * jax.nn.sigmoid(g)


def _head_ones():
    r = lax.broadcasted_iota(jnp.int32, (MXU_DIM, MXU_DIM), 0)
    c = lax.broadcasted_iota(jnp.int32, (MXU_DIM, MXU_DIM), 1)
    return jnp.where((r // RWKV_HEAD_DIM) == (c // RWKV_HEAD_DIM), 1.0, 0.0).astype(BF16)


def _headsum(x, ones_bd):
    xb = _bf(x)
    cols = [_dot(xb[:, s * MXU_DIM:(s + 1) * MXU_DIM], ones_bd) for s in range(x.shape[1] // MXU_DIM)]
    return jnp.concatenate(cols, axis=1)


def _params(*sem):
    return pltpu.CompilerParams(dimension_semantics=sem, vmem_limit_bytes=VMEM_LIMIT_BYTES)


def _even_in_kernel(h_ref, g_ref, w_ref, z_ref):
    hn = _bf(_rms(h_ref[...], g_ref[...]))
    for j in range(w_ref.shape[1] // IN_COL_TILE):
        sl = slice(j * IN_COL_TILE, (j + 1) * IN_COL_TILE)
        z_ref[:, sl] = _bf(_dot(hn, w_ref[:, sl]))


def _even_in_proj(h, gain, w_in):
    t, d = h.shape
    f = w_in.shape[1]
    tm = min(ROW_TILE, t)
    return pl.pallas_call(
        _even_in_kernel,
        out_shape=jax.ShapeDtypeStruct((t, f), BF16),
        grid=(t // tm,),
        in_specs=[
            pl.BlockSpec((tm, d), lambda i: (i, 0)),
            pl.BlockSpec((1, d), lambda i: (0, 0)),
            pl.BlockSpec((d, f), lambda i: (0, 0)),
        ],
        out_specs=pl.BlockSpec((tm, f), lambda i: (i, 0)),
        compiler_params=_params("parallel"),
        name="even_in_proj",
    )(h, gain.reshape(1, d), w_in)


def _ret_kernel(q_ref, k_ref, v_ref, g_ref, cos_ref, sin_ref, intra_ref, qdec_ref, kdec_ref, cdec_ref,
                gn_ref, o_ref, state_ref):
    @pl.when(pl.program_id(1) == 0)
    def _():
        state_ref[...] = jnp.zeros_like(state_ref)

    hd = RET_HEAD_DIM
    half = hd // 2
    cos = cos_ref[...]
    sin = sin_ref[...]
    hs = range(RET_HEADS)
    sl = [slice(h * hd, (h + 1) * hd) for h in hs]
    q = [q_ref[:, s].astype(F32) for s in sl]
    k = [k_ref[:, s].astype(F32) for s in sl]
    qr = [q[h] * cos + pltpu.roll(q[h], half, 1) * sin for h in hs]
    kr = [(k[h] * cos + pltpu.roll(k[h], half, 1) * sin) * (hd ** -0.5) for h in hs]
    v = [v_ref[:, s] for s in sl]
    scores = [_dot_nt(_bf(qr[h]), _bf(kr[h])) * intra_ref[h] for h in hs]
    state = [state_ref[h] for h in hs]
    out = [_dot(_bf(scores[h]), v[h]) + _dot(_bf(qr[h] * qdec_ref[h]), _bf(state[h])) for h in hs]
    for h in hs:
        state_ref[h] = state[h] * cdec_ref[h, 0:1, :] + _dot_tn(_bf(kr[h] * kdec_ref[h]), v[h])
    for h in hs:
        mu = jnp.mean(out[h], axis=-1, keepdims=True)
        dev = out[h] - mu
        var = jnp.mean(dev * dev, axis=-1, keepdims=True)
        y = dev * lax.rsqrt(var + GN_EPS) * gn_ref[:, sl[h]]
        o_ref[:, sl[h]] = _bf(y * _silu(g_ref[:, sl[h]].astype(F32)))


def _retention(z, batch, seq, gn_gain):
    t = z.shape[0]
    c = min(RET_CHUNK, seq)
    nc = seq // c
    hd = RET_HEAD_DIM
    half = hd // 2
    inv_freq = ROPE_BASE ** (-jnp.arange(half, dtype=F32) / half)
    ang = jnp.arange(seq, dtype=F32)[:, None] * inv_freq[None, :]
    cos2 = jnp.concatenate([jnp.cos(ang), jnp.cos(ang)], axis=-1)
    sin2 = jnp.concatenate([-jnp.sin(ang), jnp.sin(ang)], axis=-1)
    lg = jnp.log(1.0 - 2.0 ** (-5.0 - jnp.arange(RET_HEADS, dtype=F32)))
    ci = jnp.arange(c, dtype=F32)
    diff = ci[:, None] - ci[None, :]
    intra = jnp.where(diff[None] >= 0, jnp.exp(jnp.maximum(diff, 0.0)[None] * lg[:, None, None]), 0.0)
    qdec = jnp.broadcast_to(jnp.exp((ci + 1)[None, :, None] * lg[:, None, None]), (RET_HEADS, c, hd))
    kdec = jnp.broadcast_to(jnp.exp((c - 1 - ci)[None, :, None] * lg[:, None, None]), (RET_HEADS, c, hd))
    cdec = jnp.broadcast_to(jnp.exp(c * lg)[:, None, None], (RET_HEADS, SUBLANES, hd))

    width = RET_HEADS * hd

    def zcol(j):
        return pl.BlockSpec((c, width), lambda b, n: (b * nc + n, j))

    full = lambda a: pl.BlockSpec(a.shape, lambda b, n: (0,) * a.ndim)
    gn = gn_gain.reshape(1, -1)
    return pl.pallas_call(
        _ret_kernel,
        out_shape=jax.ShapeDtypeStruct((t, width), BF16),
        grid=(batch, nc),
        in_specs=[
            zcol(0), zcol(1), zcol(2), zcol(3),
            pl.BlockSpec((c, hd), lambda b, n: (n, 0)),
            pl.BlockSpec((c, hd), lambda b, n: (n, 0)),
            full(intra), full(qdec), full(kdec), full(cdec), full(gn),
        ],
        out_specs=pl.BlockSpec((c, width), lambda b, n: (b * nc + n, 0)),
        scratch_shapes=[pltpu.VMEM((RET_HEADS, hd, hd), F32)],
        compiler_params=_params("parallel", "arbitrary"),
        name="retention",
    )(z, z, z, z, cos2, sin2, intra, qdec, kdec, cdec, gn)


def _sb_kernel(q_ref, k_ref, v_ref, g_ref, o_ref, acc_ref, run_ref, *, tile):
    i = pl.program_id(1)
    hd = SB_HEAD_DIM
    scale = hd ** -0.5
    hs = range(SB_HEADS)
    cols = [slice(h * hd, (h + 1) * hd) for h in hs]
    q = [q_ref[:, s] for s in cols]
    trow = lax.broadcasted_iota(jnp.int32, (tile, tile), 0)
    tcol = lax.broadcasted_iota(jnp.int32, (tile, tile), 1)
    tri = jnp.where(trow >= tcol, 1.0, 0.0).astype(BF16)
    causal = tcol < trow

    acc_ref[...] = jnp.zeros_like(acc_ref)
    run_ref[...] = jnp.zeros_like(run_ref)

    def span(kb, diagonal):
        ks = pl.multiple_of(kb * tile, tile)
        z = [_dot_nt(q[h], k_ref[pl.ds(ks, tile), cols[h]]) * scale for h in hs]
        l1mb = [-_softplus(z[h]) for h in hs]
        if diagonal:
            l1mb = [jnp.where(causal, l1mb[h], 0.0) for h in hs]
        parts = [_split(l1mb[h]) for h in hs]
        both = [_dot(jnp.concatenate(parts[h], axis=0), tri) for h in hs]
        cs = [both[h][:tile] + both[h][tile:] for h in hs]
        run = [run_ref[h] for h in hs]
        w = [jnp.exp(z[h] + cs[h] + jnp.concatenate([run[h]] * (tile // LANES), axis=1)) for h in hs]
        if diagonal:
            w = [jnp.where(causal, w[h], 0.0) for h in hs]
        for h in hs:
            acc_ref[h] += _dot(_bf(w[h]), v_ref[pl.ds(ks, tile), cols[h]])
            run_ref[h] = run[h] + jnp.broadcast_to(cs[h][:, 0:1], run[h].shape)

    def alive():
        return (jnp.max(run_ref[...]) >= F32_EXP_UNDERFLOW_LOG).astype(jnp.int32)

    span(i, True)

    def body(carry):
        j, _ = carry
        span(i - 1 - j, False)
        return j + 1, alive()

    lax.while_loop(lambda c: (c[0] < i) & (c[1] > 0), body, (jnp.int32(0), alive()))
    for h in hs:
        o_ref[:, cols[h]] = _bf(acc_ref[h] * _silu(g_ref[:, cols[h]].astype(F32)))


def _stick_breaking(z, batch, seq):
    t = z.shape[0]
    width = SB_HEADS * SB_HEAD_DIM
    tile = min(SB_TILE, seq)
    nq = seq // tile
    base = 4
    return pl.pallas_call(
        functools.partial(_sb_kernel, tile=tile),
        out_shape=jax.ShapeDtypeStruct((t, width), BF16),
        grid=(batch, nq),
        in_specs=[
            pl.BlockSpec((tile, width), lambda b, i: (b * nq + i, base)),
            pl.BlockSpec((seq, width), lambda b, i: (b, base + 1)),
            pl.BlockSpec((seq, width), lambda b, i: (b, base + 2)),
            pl.BlockSpec((tile, width), lambda b, i: (b * nq + i, base + 3)),
        ],
        out_specs=pl.BlockSpec((tile, width), lambda b, i: (b * nq + i, 0)),
        scratch_shapes=[pltpu.VMEM((SB_HEADS, tile, SB_HEAD_DIM), F32), pltpu.VMEM((SB_HEADS, tile, LANES), F32)],
        compiler_params=_params("parallel", "arbitrary"),
        name="stick_breaking",
    )(z, z, z, z)


def _ple_tail(h1, p_ref, wg_ref, wp_ref, rest, final):
    gate = jax.nn.sigmoid(_dot(_bf(h1), wg_ref[...]))
    h2 = h1 + gate * _dot(_bf(p_ref[...]), wp_ref[...])
    if final:
        fg_ref, o_ref = rest
        o_ref[...] = _rms(h2, fg_ref[...])
    else:
        (o_ref,) = rest
        o_ref[...] = h2


def _even_out_kernel(h_ref, ma_ref, mb_ref, wa_ref, wb_ref, p_ref, wg_ref, wp_ref, *rest, final):
    h1 = h_ref[...] + _dot(ma_ref[...], wa_ref[...]) + _dot(mb_ref[...], wb_ref[...])
    _ple_tail(h1, p_ref, wg_ref, wp_ref, rest, final)


def _odd_out_kernel(h_ref, y_ref, bonus_ref, gs_ref, lg_ref, lb_ref, w_ref, p_ref, wg_ref, wp_ref, *rest, final):
    ones_bd = _head_ones()
    inv_n = 1.0 / RWKV_HEAD_DIM
    y = y_ref[...]
    dev = y - _headsum(y, ones_bd) * inv_n
    var = _headsum(dev * dev, ones_bd) * inv_n
    yn = dev * lax.rsqrt(var + LNX_EPS) * lg_ref[...] + lb_ref[...]
    mixed = _bf((yn + bonus_ref[...]) * gs_ref[...].astype(F32))
    _ple_tail(h_ref[...] + _dot(mixed, w_ref[...]), p_ref, wg_ref, wp_ref, rest, final)


def _out_call(kern, name, h, row_args, full_args_mid, p, w_gate, w_ple, final_gain):
    t, d = h.shape
    tm = min(ROW_TILE, t)
    final = final_gain is not None
    row = lambda a: pl.BlockSpec((tm, a.shape[1]), lambda i: (i, 0))
    full = lambda a: pl.BlockSpec(a.shape, lambda i: (0,) * a.ndim)
    args = [h, *row_args, *full_args_mid, p, w_gate, w_ple]
    specs = [row(h)] + [row(a) for a in row_args] + [full(a) for a in full_args_mid] + [row(p), full(w_gate),
                                                                                        full(w_ple)]
    if final:
        args.append(final_gain.reshape(1, d))
        specs.append(full(args[-1]))
    return pl.pallas_call(
        functools.partial(kern, final=final),
        out_shape=jax.ShapeDtypeStruct((t, d), F32),
        grid=(t // tm,),
        in_specs=specs,
        out_specs=row(h),
        compiler_params=_params("parallel"),
        name=name,
    )(*args)


def _odd_in_kernel(*refs, vres, tiles_per_seq, chunk, sub):
    (h_ref, hp_ref, gain_ref, mu_ref, win_ref, w0_ref, w1_ref, w2_ref, a0_ref, a1_ref, a2_ref,
     kk_ref, ka_ref, rk_ref) = refs[:14]
    if vres:
        v0_ref, v1_ref, v2_ref, vf_ref = refs[14:18]
        outs = refs[18:]
    else:
        outs = refs[14:]
    rd_ref, kkd_ref, kinv_ref, binv_ref, kt_ref, bt_ref, vb_ref, wc_ref, bonus_ref, gs_ref = outs[:10]

    gain = gain_ref[...]
    hn_all = _rms(h_ref[...], gain)
    hnp = _rms(hp_ref[...], gain)
    tm = hn_all.shape[0]
    first = (pl.program_id(0) % tiles_per_seq) == 0
    prev_row = jnp.where(first, 0.0, hnp[SUBLANES - 1:SUBLANES, :])
    rowid = lax.broadcasted_iota(jnp.int32, hn_all.shape, 0)
    xx_all = jnp.where(rowid == 0, prev_row, pltpu.roll(hn_all, 1, 0)) - hn_all
    ones_bd = _head_ones()
    ri = lax.broadcasted_iota(jnp.int32, (sub, sub), 0)
    ci = lax.broadcasted_iota(jnp.int32, (sub, sub), 1)
    sel = jnp.where(((ri // chunk) == (ci // chunk)) & (ri >= ci), 1.0, 0.0).astype(BF16)

    def part(r0):
        rows = slice(r0, r0 + sub)
        hn, xx = hn_all[rows], xx_all[rows]
        mix = lambda p: hn + xx * mu_ref[p:p + 1, :]
        r = _dot(_bf(mix(0)), win_ref[0])
        yield
        k = _dot(_bf(mix(1)), win_ref[1])
        yield
        xv = _bf(mix(2))
        v = _dot(xv, win_ref[2])
        yield
        g = _dot(_bf(mix(3)), win_ref[3])
        yield
        xw = _bf(mix(4))
        xa = _bf(mix(5))
        w_log = -_softplus(-(w0_ref[...] + _dot(_bf(jnp.tanh(_dot(xw, w1_ref[...]))), w2_ref[...]))) - 0.5
        lw = -jnp.exp(w_log)
        a = jax.nn.sigmoid(a0_ref[...] + _dot(_bf(_dot(xa, a1_ref[...])), a2_ref[...]))
        if vres:
            mixv = jax.nn.sigmoid(v0_ref[...] + _dot(_bf(_dot(xv, v1_ref[...])), v2_ref[...]))
            v = v + (vf_ref[rows, :] - v) * mixv
        else:
            outs[10][rows, :] = v
        yield
        kkraw = k * kk_ref[...]
        kk = kkraw * lax.rsqrt(jnp.maximum(_headsum(kkraw * kkraw, ones_bd), 1e-24))
        kmod = k * (1.0 + (a - 1.0) * ka_ref[...])
        b = kk * a
        hi, lo = _split(lw)
        cum = _dot(sel, hi) + _dot(sel, lo)
        ends = [jnp.exp(cum[(c + 1) * chunk - 1:(c + 1) * chunk, :]) for c in range(sub // chunk)]
        wc = jnp.concatenate([jnp.broadcast_to(e, (chunk, e.shape[1])) for e in ends], axis=0)
        yield
        w_inv = jnp.exp(-cum)
        kinv = kmod * w_inv
        binv = b * w_inv
        rd_ref[rows, :] = _bf(r * jnp.exp(cum))
        kkd_ref[rows, :] = _bf(kk * jnp.exp(cum - lw))
        kinv_ref[rows, :] = _bf(kinv)
        binv_ref[rows, :] = _bf(binv)
        yield
        kt_ref[rows, :] = _bf(kinv * wc)
        bt_ref[rows, :] = _bf(-(binv * wc))
        vb_ref[rows, :] = _bf(v)
        for c in range(sub // chunk):
            wc_ref[r0 // chunk + c] = ends[c]
        bonus_ref[rows, :] = _headsum(r * kmod * rk_ref[...], ones_bd) * v
        gs_ref[rows, :] = _bf(_silu(g))
        yield

    parts = [part(r0) for r0 in range(0, tm, sub)]
    for _ in range(4):
        next(parts[0])
    for p in range(1, len(parts)):
        for _ in zip(parts[p - 1], parts[p]):
            pass
    for _ in parts[-1]:
        pass


def _odd_in_proj(h, seq, gain, mu, w_in, w0, w1, w2, a0, a1, a2, k_k, k_a, r_k, vres=None, v_first=None):
    t, d = h.shape
    tm = min(ODD_ROW_TILE, seq)
    chunk = SCAN_CHUNK
    tiles_per_seq = seq // tm
    sub_per_tile = tm // SUBLANES
    row = pl.BlockSpec((tm, d), lambda i: (i, 0))
    full = lambda a: pl.BlockSpec(a.shape, lambda i: (0,) * a.ndim)
    vec = lambda a: a.reshape(1, -1)
    args = [h, h, vec(gain), mu, w_in, vec(w0), w1, w2, vec(a0), a1, a2, vec(k_k), vec(k_a), vec(r_k)]
    specs = [row, pl.BlockSpec((SUBLANES, d), lambda i: (jnp.maximum(i * sub_per_tile - 1, 0), 0))]
    specs += [full(a) for a in args[2:]]
    if vres is not None:
        v0, v1, v2 = vres
        extra = [vec(v0), v1, v2]
        args += extra + [v_first]
        specs += [full(a) for a in extra] + [row]
    act = lambda dt: jax.ShapeDtypeStruct((t, d), dt)
    out_shape = [act(BF16)] * 7 + [jax.ShapeDtypeStruct((t // chunk, 1, d), F32), act(F32), act(BF16)]
    out_specs = [row] * 7 + [pl.BlockSpec((tm // chunk, 1, d), lambda i: (i, 0, 0)), row, row]
    if vres is None:
        out_shape.append(act(F32))
        out_specs.append(row)
    return pl.pallas_call(
        functools.partial(_odd_in_kernel, vres=vres is not None, tiles_per_seq=tiles_per_seq, chunk=chunk,
                          sub=min(ODD_SUB_ROWS, tm)),
        out_shape=out_shape,
        grid=(t // tm,),
        in_specs=specs,
        out_specs=out_specs,
        compiler_params=_params("parallel"),
        name="odd_in_proj",
    )(*args)


def _scan_kernel(rd_ref, kkd_ref, kinv_ref, binv_ref, kt_ref, bt_ref, v_ref, wc_ref, y_ref, state_ref, *, groups,
                 chunks):
    @pl.when(pl.program_id(0) == 0)
    def _():
        state_ref[...] = jnp.zeros_like(state_ref)

    batch = rd_ref.shape[0]
    c = rd_ref.shape[1] // chunks
    w = MXU_DIM
    hd = RWKV_HEAD_DIM
    row = lax.broadcasted_iota(jnp.int32, (c, w), 0)
    lane = lax.broadcasted_iota(jnp.int32, (c, w), 1)
    j = lane & (hd - 1)
    strict, incl, eye = j < row, j <= row, j == row
    blk16 = (j >> 4) == (row >> 4)
    blk32 = (j >> 5) == (row >> 5)
    head_of_lane = lane // hd
    head_masks = [head_of_lane == hh for hh in range(HEADS_PER_GROUP)]
    brow = lax.broadcasted_iota(jnp.int32, (w, w), 0)
    bcol = lax.broadcasted_iota(jnp.int32, (w, w), 1)
    bd = (brow // hd) == (bcol // hd)
    zero = jnp.zeros((), BF16)

    def stack(x):
        return jnp.concatenate([jnp.where(m, x, zero) for m in head_masks], axis=0)

    def pdot(x, y):
        return _dot(_bf(x), stack(_bf(y)))

    cols = [slice(g * w, (g + 1) * w) for g in range(groups)]
    rows = [slice(ch * c, (ch + 1) * c) for ch in range(chunks)]
    lanes = [(b, s) for b in range(batch) for s in cols]
    ix = [(b, r, s) for r in rows for b, s in lanes]
    ks = range(len(ix))
    rd = [rd_ref[i] for i in ix]
    kkd = [kkd_ref[i] for i in ix]
    v = [v_ref[i] for i in ix]
    lhs2 = [jnp.concatenate([kkd[k], rd[k]], axis=0) for k in ks]
    ak = [_dot_nt(lhs2[k], stack(kinv_ref[ix[k]])) for k in ks]
    ab = [_dot_nt(lhs2[k], stack(binv_ref[ix[k]])) for k in ks]
    a_kk = [_bf(jnp.where(strict, ak[k][:c], 0.0)) for k in ks]
    a_rk = [_bf(jnp.where(incl, ak[k][c:], 0.0)) for k in ks]
    n = [jnp.where(strict, ab[k][:c], 0.0) for k in ks]
    a_rb = [_bf(jnp.where(incl, ab[k][c:], 0.0)) for k in ks]

    nd = [jnp.where(blk16, n[k], 0.0) for k in ks]
    n2 = [pdot(nd[k], nd[k]) for k in ks]
    tinv = [jnp.where(eye, 1.0, 0.0) - nd[k] for k in ks]
    r1 = [pdot(jnp.concatenate([tinv[k], n2[k]], axis=0), n2[k]) for k in ks]
    tinv = [tinv[k] + r1[k][:c] for k in ks]
    r2 = [pdot(jnp.concatenate([tinv[k], r1[k][c:]], axis=0), r1[k][c:]) for k in ks]
    tinv = [tinv[k] + r2[k][:c] for k in ks]
    tinv = [tinv[k] + pdot(tinv[k], r2[k][c:]) for k in ks]
    off1 = [jnp.where(blk32 & jnp.logical_not(blk16), n[k], 0.0) for k in ks]
    m1 = [pdot(off1[k], tinv[k]) for k in ks]
    tinv = [tinv[k] - pdot(tinv[k], m1[k]) for k in ks]
    off2 = [jnp.where(blk32, 0.0, n[k]) for k in ks]
    m2 = [pdot(off2[k], tinv[k]) for k in ks]
    tinv = [_bf(tinv[k] - pdot(tinv[k], m2[k])) for k in ks]
    av = [_dot(jnp.concatenate([a_kk[k], a_rk[k]], axis=0), stack(v[k])) for k in ks]

    gs = range(len(lanes))
    state = [state_ref[g] for g in gs]
    for ch in range(chunks):
        chain = [ch * len(lanes) + g for g in gs]
        ps = [_dot_nt(lhs2[k], _bf(state[g])) for g, k in zip(gs, chain)]
        skb = [_bf(_dot(tinv[k], stack(_bf(ps[g][:c] + av[k][:c])))) for g, k in zip(gs, chain)]
        for g, k in zip(gs, chain):
            y_ref[ix[k]] = ps[g][c:] + av[k][c:] - _dot(a_rb[k], stack(skb[g]))
        upd = [_dot_tn(jnp.concatenate([v[k], skb[g]], axis=0),
                       jnp.concatenate([kt_ref[ix[k]], bt_ref[ix[k]]], axis=0))
               for g, k in zip(gs, chain)]
        state = [state[g] * wc_ref[lanes[g][0], ch, :, lanes[g][1]] + jnp.where(bd, upd[g], 0.0) for g in gs]
    for g in gs:
        state_ref[g] = state[g]


def _rwkv_scan(rd, kkd, kinv, binv, ktail, btail, vb, wc, batch, seq):
    t, d = rd.shape
    chunks = SCAN_CHUNKS_PER_STEP
    rows = SCAN_CHUNK * chunks
    groups = d // MXU_DIM
    blk = pl.BlockSpec((batch, rows, d), lambda n: (0, n, 0))
    seqs = [a.reshape(batch, seq, d) for a in (rd, kkd, kinv, binv, ktail, btail, vb)]
    y = pl.pallas_call(
        functools.partial(_scan_kernel, groups=groups, chunks=chunks),
        out_shape=jax.ShapeDtypeStruct((batch, seq, d), F32),
        grid=(seq // rows,),
        in_specs=[blk] * 7 + [pl.BlockSpec((batch, chunks, 1, d), lambda n: (0, n, 0, 0))],
        out_specs=blk,
        scratch_shapes=[pltpu.VMEM((batch * groups, MXU_DIM, MXU_DIM), F32)],
        compiler_params=_params("arbitrary"),
        name="rwkv7_scan",
    )(*seqs, wc.reshape(batch, seq // SCAN_CHUNK, 1, d))
    return y.reshape(t, d)


def kernel(x, p, norm_gain, final_gain, ple_proj, ple_gate, even_w_in, even_w_out, ret_gn_gain, odd_mu, odd_w_in, odd_w_out, rwkv_w0, rwkv_w1, rwkv_w2, rwkv_a0, rwkv_a1, rwkv_a2, rwkv_v0, rwkv_v1, rwkv_v2, rwkv_k_k, rwkv_k_a, rwkv_r_k, rwkv_lnx_gain, rwkv_lnx_bias):
    batch, seq, d = x.shape
    depth = p.shape[0]
    t = batch * seq
    h = x.reshape(t, d)
    p2 = p.reshape(depth, t, p.shape[-1])
    ret_w = RET_HEADS * RET_HEAD_DIM
    vec = lambda a: a.reshape(1, -1)
    v_first = None
    for i in range(depth):
        final = final_gain if i == depth - 1 else None
        w_gate, w_ple = _bf(ple_gate[i]), _bf(ple_proj[i])
        if i % 2 == 0:
            e = i // 2
            z = _even_in_proj(h, norm_gain[i], _bf(even_w_in[e]))
            mixed_a = _retention(z, batch, seq, ret_gn_gain[e])
            mixed_b = _stick_breaking(z, batch, seq)
            w_out = _bf(even_w_out[e])
            h = _out_call(_even_out_kernel, "even_out_proj_ple", h, [mixed_a, mixed_b],
                          [w_out[:ret_w], w_out[ret_w:]], p2[i], w_gate, w_ple, final)
        else:
            o = i // 2
            vres = None if v_first is None else (rwkv_v0[o - 1], _bf(rwkv_v1[o - 1]), _bf(rwkv_v2[o - 1]))
            outs = _odd_in_proj(
                h, seq, norm_gain[i], odd_mu[o], _bf(odd_w_in[o]), rwkv_w0[o], _bf(rwkv_w1[o]), _bf(rwkv_w2[o]),
                rwkv_a0[o], _bf(rwkv_a1[o]), _bf(rwkv_a2[o]), rwkv_k_k[o], rwkv_k_a[o], rwkv_r_k[o], vres, v_first)
            rd, kkd, kinv, binv, ktail, btail, vb, wc, bonus, gs = outs[:10]
            if v_first is None:
                v_first = outs[10]
            y = _rwkv_scan(rd, kkd, kinv, binv, ktail, btail, vb, wc, batch, seq)
            h = _out_call(_odd_out_kernel, "odd_out_proj_ple", h, [y, bonus, gs],
                          [vec(rwkv_lnx_gain[o]), vec(rwkv_lnx_bias[o]), _bf(odd_w_out[o])], p2[i], w_gate, w_ple,
                          final)
    return h.reshape(batch, seq, d)
```

```python
import functools

import jax
import jax.numpy as jnp
import numpy as np
from jax import lax
from jax.experimental import pallas as pl
from jax.experimental.pallas import tpu as pltpu

F32 = jnp.float32
BF16 = jnp.bfloat16

LANES = 128
SUBLANES = 8
MXU_DIM = 256
VMEM_LIMIT_BYTES = 56 * 1024 * 1024

RET_HEADS = 4
RET_HEAD_DIM = 128
SB_HEADS = 4
SB_HEAD_DIM = 128
ROPE_BASE = 10000.0
RWKV_HEAD_DIM = 64
RMS_EPS = 1e-6
GN_EPS = 1e-5
LNX_EPS = 64e-5
F32_EXP_UNDERFLOW_LOG = -110.0

ROW_TILE = 512
ODD_ROW_TILE = 512
ODD_SUB_ROWS = 256
IN_COL_TILE = 1024
RET_CHUNK = 256
SB_TILE = 256
SCAN_CHUNK = 64
SCAN_CHUNKS_PER_STEP = 4
HEADS_PER_GROUP = MXU_DIM // RWKV_HEAD_DIM


def _bf(x):
    return x.astype(BF16)


def _dot(a, b):
    return jnp.dot(a, b, preferred_element_type=F32)


def _dot_nt(a, b):
    return lax.dot_general(a, b, (((1,), (1,)), ((), ())), preferred_element_type=F32)


def _dot_tn(a, b):
    return lax.dot_general(a, b, (((0,), (0,)), ((), ())), preferred_element_type=F32)


def _split(x):
    hi = _bf(x)
    lo = _bf(x - hi.astype(F32))
    return hi, lo


def _rms(x, gain):
    return x * lax.rsqrt(jnp.mean(x * x, axis=-1, keepdims=True) + RMS_EPS) * gain


def _softplus(y):
    return jnp.maximum(y, 0.0) + jnp.log(1.0 + jnp.exp(-jnp.abs(y)))


def _silu(g):
    return g * jax.nn.sigmoid(g)


def _head_ones():
    r = lax.broadcasted_iota(jnp.int32, (MXU_DIM, MXU_DIM), 0)
    c = lax.broadcasted_iota(jnp.int32, (MXU_DIM, MXU_DIM), 1)
    return jnp.where((r // RWKV_HEAD_DIM) == (c // RWKV_HEAD_DIM), 1.0, 0.0).astype(BF16)


def _headsum(x, ones_bd):
    xb = _bf(x)
    cols = [_dot(xb[:, s * MXU_DIM:(s + 1) * MXU_DIM], ones_bd) for s in range(x.shape[1] // MXU_DIM)]
    return jnp.concatenate(cols, axis=1)


def _params(*sem):
    return pltpu.CompilerParams(dimension_semantics=sem, vmem_limit_bytes=VMEM_LIMIT_BYTES)


def _even_in_kernel(h_ref, g_ref, w_ref, z_ref):
    hn = _bf(_rms(h_ref[...], g_ref[...]))
    for j in range(w_ref.shape[1] // IN_COL_TILE):
        sl = slice(j * IN_COL_TILE, (j + 1) * IN_COL_TILE)
        z_ref[:, sl] = _bf(_dot(hn, w_ref[:, sl]))


def _even_in_proj(h, gain, w_in):
    t, d = h.shape
    f = w_in.shape[1]
    tm = min(ROW_TILE, t)
    return pl.pallas_call(
        _even_in_kernel,
        out_shape=jax.ShapeDtypeStruct((t, f), BF16),
        grid=(t // tm,),
        in_specs=[
            pl.BlockSpec((tm, d), lambda i: (i, 0)),
            pl.BlockSpec((1, d), lambda i: (0, 0)),
            pl.BlockSpec((d, f), lambda i: (0, 0)),
        ],
        out_specs=pl.BlockSpec((tm, f), lambda i: (i, 0)),
        compiler_params=_params("parallel"),
        name="even_in_proj",
    )(h, gain.reshape(1, d), w_in)


def _ret_kernel(q_ref, k_ref, v_ref, g_ref, cos_ref, sin_ref, intra_ref, qdec_ref, kdec_ref, cdec_ref,
                gn_ref, o_ref, state_ref):
    @pl.when(pl.program_id(1) == 0)
    def _():
        state_ref[...] = jnp.zeros_like(state_ref)

    hd = RET_HEAD_DIM
    half = hd // 2
    cos = cos_ref[...]
    sin = sin_ref[...]
    hs = range(RET_HEADS)
    sl = [slice(h * hd, (h + 1) * hd) for h in hs]
    q = [q_ref[:, s].astype(F32) for s in sl]
    k = [k_ref[:, s].astype(F32) for s in sl]
    qr = [q[h] * cos + pltpu.roll(q[h], half, 1) * sin for h in hs]
    kr = [(k[h] * cos + pltpu.roll(k[h], half, 1) * sin) * (hd ** -0.5) for h in hs]
    v = [v_ref[:, s] for s in sl]
    scores = [_dot_nt(_bf(qr[h]), _bf(kr[h])) * intra_ref[h] for h in hs]
    state = [state_ref[h] for h in hs]
    out = [_dot(_bf(scores[h]), v[h]) + _dot(_bf(qr[h] * qdec_ref[h]), _bf(state[h])) for h in hs]
    for h in hs:
        state_ref[h] = state[h] * cdec_ref[h, 0:1, :] + _dot_tn(_bf(kr[h] * kdec_ref[h]), v[h])
    for h in hs:
        mu = jnp.mean(out[h], axis=-1, keepdims=True)
        dev = out[h] - mu
        var = jnp.mean(dev * dev, axis=-1, keepdims=True)
        y = dev * lax.rsqrt(var + GN_EPS) * gn_ref[:, sl[h]]
        o_ref[:, sl[h]] = _bf(y * _silu(g_ref[:, sl[h]].astype(F32)))


def _retention(z, batch, seq, gn_gain):
    t = z.shape[0]
    c = min(RET_CHUNK, seq)
    nc = seq // c
    hd = RET_HEAD_DIM
    half = hd // 2
    f32 = np.float32
    inv_freq = f32(ROPE_BASE) ** (-np.arange(half, dtype=f32) / f32(half))
    ang = np.arange(seq, dtype=f32)[:, None] * inv_freq[None, :]
    cos2 = np.concatenate([np.cos(ang), np.cos(ang)], axis=-1)
    sin2 = np.concatenate([-np.sin(ang), np.sin(ang)], axis=-1)
    lg = np.log(f32(1.0) - f32(2.0) ** (f32(-5.0) - np.arange(RET_HEADS, dtype=f32)))
    ci = np.arange(c, dtype=f32)
    diff = ci[:, None] - ci[None, :]
    intra = np.where(diff[None] >= 0, np.exp(np.maximum(diff, f32(0.0))[None] * lg[:, None, None]), f32(0.0))
    qdec = np.ascontiguousarray(np.broadcast_to(np.exp((ci + 1)[None, :, None] * lg[:, None, None]),
                                                (RET_HEADS, c, hd)))
    kdec = np.ascontiguousarray(np.broadcast_to(np.exp((c - 1 - ci)[None, :, None] * lg[:, None, None]),
                                                (RET_HEADS, c, hd)))
    cdec = np.ascontiguousarray(np.broadcast_to(np.exp(f32(c) * lg)[:, None, None], (RET_HEADS, SUBLANES, hd)))
    cos2, sin2, intra, qdec, kdec, cdec = (jnp.asarray(a, F32) for a in (cos2, sin2, intra, qdec, kdec, cdec))

    width = RET_HEADS * hd

    def zcol(j):
        return pl.BlockSpec((c, width), lambda b, n: (b * nc + n, j))

    full = lambda a: pl.BlockSpec(a.shape, lambda b, n: (0,) * a.ndim)
    gn = gn_gain.reshape(1, -1)
    return pl.pallas_call(
        _ret_kernel,
        out_shape=jax.ShapeDtypeStruct((t, width), BF16),
        grid=(batch, nc),
        in_specs=[
            zcol(0), zcol(1), zcol(2), zcol(3),
            pl.BlockSpec((c, hd), lambda b, n: (n, 0)),
            pl.BlockSpec((c, hd), lambda b, n: (n, 0)),
            full(intra), full(qdec), full(kdec), full(cdec), full(gn),
        ],
        out_specs=pl.BlockSpec((c, width), lambda b, n: (b * nc + n, 0)),
        scratch_shapes=[pltpu.VMEM((RET_HEADS, hd, hd), F32)],
        compiler_params=_params("parallel", "arbitrary"),
        name="retention",
    )(z, z, z, z, cos2, sin2, intra, qdec, kdec, cdec, gn)


def _sb_kernel(q_ref, k_ref, v_ref, g_ref, o_ref, acc_ref, run_ref, *, tile):
    i = pl.program_id(1)
    hd = SB_HEAD_DIM
    scale = hd ** -0.5
    hs = range(SB_HEADS)
    cols = [slice(h * hd, (h + 1) * hd) for h in hs]
    q = [q_ref[:, s] for s in cols]
    trow = lax.broadcasted_iota(jnp.int32, (tile, tile), 0)
    tcol = lax.broadcasted_iota(jnp.int32, (tile, tile), 1)
    tri = jnp.where(trow >= tcol, 1.0, 0.0).astype(BF16)
    causal = tcol < trow

    acc_ref[...] = jnp.zeros_like(acc_ref)
    run_ref[...] = jnp.zeros_like(run_ref)

    def span(kb, diagonal):
        ks = pl.multiple_of(kb * tile, tile)
        z = [_dot_nt(q[h], k_ref[pl.ds(ks, tile), cols[h]]) * scale for h in hs]
        l1mb = [-_softplus(z[h]) for h in hs]
        if diagonal:
            l1mb = [jnp.where(causal, l1mb[h], 0.0) for h in hs]
        parts = [_split(l1mb[h]) for h in hs]
        both = [_dot(jnp.concatenate(parts[h], axis=0), tri) for h in hs]
        cs = [both[h][:tile] + both[h][tile:] for h in hs]
        run = [run_ref[h] for h in hs]
        w = [jnp.exp(z[h] + cs[h] + jnp.concatenate([run[h]] * (tile // LANES), axis=1)) for h in hs]
        if diagonal:
            w = [jnp.where(causal, w[h], 0.0) for h in hs]
        for h in hs:
            acc_ref[h] += _dot(_bf(w[h]), v_ref[pl.ds(ks, tile), cols[h]])
            run_ref[h] = run[h] + jnp.broadcast_to(cs[h][:, 0:1], run[h].shape)

    def alive():
        return (jnp.max(run_ref[...]) >= F32_EXP_UNDERFLOW_LOG).astype(jnp.int32)

    span(i, True)

    def body(carry):
        j, _ = carry
        span(i - 1 - j, False)
        return j + 1, alive()

    lax.while_loop(lambda c: (c[0] < i) & (c[1] > 0), body, (jnp.int32(0), alive()))
    for h in hs:
        o_ref[:, cols[h]] = _bf(acc_ref[h] * _silu(g_ref[:, cols[h]].astype(F32)))


def _stick_breaking(z, batch, seq):
    t = z.shape[0]
    width = SB_HEADS * SB_HEAD_DIM
    tile = min(SB_TILE, seq)
    nq = seq // tile
    base = 4
    return pl.pallas_call(
        functools.partial(_sb_kernel, tile=tile),
        out_shape=jax.ShapeDtypeStruct((t, width), BF16),
        grid=(batch, nq),
        in_specs=[
            pl.BlockSpec((tile, width), lambda b, i: (b * nq + i, base)),
            pl.BlockSpec((seq, width), lambda b, i: (b, base + 1)),
            pl.BlockSpec((seq, width), lambda b, i: (b, base + 2)),
            pl.BlockSpec((tile, width), lambda b, i: (b * nq + i, base + 3)),
        ],
        out_specs=pl.BlockSpec((tile, width), lambda b, i: (b * nq + i, 0)),
        scratch_shapes=[pltpu.VMEM((SB_HEADS, tile, SB_HEAD_DIM), F32), pltpu.VMEM((SB_HEADS, tile, LANES), F32)],
        compiler_params=_params("parallel", "arbitrary"),
        name="stick_breaking",
    )(z, z, z, z)


def _ple_tail(h1, p_ref, wg_ref, wp_ref, rest, final):
    gate = jax.nn.sigmoid(_dot(_bf(h1), wg_ref[...]))
    h2 = h1 + gate * _dot(_bf(p_ref[...]), wp_ref[...])
    if final:
        fg_ref, o_ref = rest
        o_ref[...] = _rms(h2, fg_ref[...])
    else:
        (o_ref,) = rest
        o_ref[...] = h2


def _even_out_kernel(h_ref, ma_ref, mb_ref, w_ref, p_ref, wg_ref, wp_ref, *rest, final):
    wa = ma_ref.shape[1]
    h1 = h_ref[...] + _dot(ma_ref[...], w_ref[:wa, :]) + _dot(mb_ref[...], w_ref[wa:, :])
    _ple_tail(h1, p_ref, wg_ref, wp_ref, rest, final)


def _odd_out_kernel(h_ref, y_ref, bonus_ref, gs_ref, lg_ref, lb_ref, w_ref, p_ref, wg_ref, wp_ref, *rest, final):
    ones_bd = _head_ones()
    inv_n = 1.0 / RWKV_HEAD_DIM
    y = y_ref[...]
    dev = y - _headsum(y, ones_bd) * inv_n
    var = _headsum(dev * dev, ones_bd) * inv_n
    yn = dev * lax.rsqrt(var + LNX_EPS) * lg_ref[...] + lb_ref[...]
    mixed = _bf((yn + bonus_ref[...]) * gs_ref[...].astype(F32))
    _ple_tail(h_ref[...] + _dot(mixed, w_ref[...]), p_ref, wg_ref, wp_ref, rest, final)


def _out_call(kern, name, h, row_args, full_args_mid, p, layer, w_gate, w_ple, final_gain):
    t, d = h.shape
    tm = min(ROW_TILE, t)
    final = final_gain is not None
    row = lambda a: pl.BlockSpec((tm, a.shape[1]), lambda i: (i, 0))
    full = lambda a: pl.BlockSpec(a.shape, lambda i: (0,) * a.ndim)
    args = [h, *row_args, *full_args_mid, p, w_gate, w_ple]
    p_spec = pl.BlockSpec((None, tm, p.shape[2]), lambda i: (layer, i, 0))
    specs = [row(h)] + [row(a) for a in row_args] + [full(a) for a in full_args_mid] + [p_spec, full(w_gate),
                                                                                        full(w_ple)]
    if final:
        args.append(final_gain.reshape(1, d))
        specs.append(full(args[-1]))
    return pl.pallas_call(
        functools.partial(kern, final=final),
        out_shape=jax.ShapeDtypeStruct((t, d), F32),
        grid=(t // tm,),
        in_specs=specs,
        out_specs=row(h),
        compiler_params=_params("parallel"),
        name=name,
    )(*args)


def _odd_in_kernel(*refs, vres, tiles_per_seq, chunk, sub):
    (h_ref, hp_ref, gain_ref, mu_ref, win_ref, w0_ref, w1_ref, w2_ref, a0_ref, a1_ref, a2_ref,
     kk_ref, ka_ref, rk_ref) = refs[:14]
    if vres:
        v0_ref, v1_ref, v2_ref, vf_ref = refs[14:18]
        outs = refs[18:]
    else:
        outs = refs[14:]
    rd_ref, kkd_ref, kinv_ref, binv_ref, kt_ref, bt_ref, vb_ref, wc_ref, bonus_ref, gs_ref = outs[:10]

    gain = gain_ref[...]
    hn_all = _rms(h_ref[...], gain)
    hnp = _rms(hp_ref[...], gain)
    tm = hn_all.shape[0]
    first = (pl.program_id(0) % tiles_per_seq) == 0
    prev_row = jnp.where(first, 0.0, hnp[SUBLANES - 1:SUBLANES, :])
    rowid = lax.broadcasted_iota(jnp.int32, hn_all.shape, 0)
    xx_all = jnp.where(rowid == 0, prev_row, pltpu.roll(hn_all, 1, 0)) - hn_all
    ones_bd = _head_ones()
    ri = lax.broadcasted_iota(jnp.int32, (sub, sub), 0)
    ci = lax.broadcasted_iota(jnp.int32, (sub, sub), 1)
    sel = jnp.where(((ri // chunk) == (ci // chunk)) & (ri >= ci), 1.0, 0.0).astype(BF16)

    def part(r0):
        rows = slice(r0, r0 + sub)
        hn, xx = hn_all[rows], xx_all[rows]
        mix = lambda p: hn + xx * mu_ref[p:p + 1, :]
        r = _dot(_bf(mix(0)), win_ref[0])
        yield
        k = _dot(_bf(mix(1)), win_ref[1])
        yield
        xv = _bf(mix(2))
        v = _dot(xv, win_ref[2])
        yield
        g = _dot(_bf(mix(3)), win_ref[3])
        yield
        xw = _bf(mix(4))
        xa = _bf(mix(5))
        w_log = -_softplus(-(w0_ref[...] + _dot(_bf(jnp.tanh(_dot(xw, w1_ref[...]))), w2_ref[...]))) - 0.5
        lw = -jnp.exp(w_log)
        a = jax.nn.sigmoid(a0_ref[...] + _dot(_bf(_dot(xa, a1_ref[...])), a2_ref[...]))
        if vres:
            mixv = jax.nn.sigmoid(v0_ref[...] + _dot(_bf(_dot(xv, v1_ref[...])), v2_ref[...]))
            v = v + (vf_ref[rows, :] - v) * mixv
        else:
            outs[10][rows, :] = v
        yield
        kkraw = k * kk_ref[...]
        kk = kkraw * lax.rsqrt(jnp.maximum(_headsum(kkraw * kkraw, ones_bd), 1e-24))
        kmod = k * (1.0 + (a - 1.0) * ka_ref[...])
        b = kk * a
        hi, lo = _split(lw)
        cum = _dot(sel, hi) + _dot(sel, lo)
        ends = [jnp.exp(cum[(c + 1) * chunk - 1:(c + 1) * chunk, :]) for c in range(sub // chunk)]
        wc = jnp.concatenate([jnp.broadcast_to(e, (chunk, e.shape[1])) for e in ends], axis=0)
        yield
        w_inv = jnp.exp(-cum)
        kinv = kmod * w_inv
        binv = b * w_inv
        rd_ref[rows, :] = _bf(r * jnp.exp(cum))
        kkd_ref[rows, :] = _bf(kk * jnp.exp(cum - lw))
        kinv_ref[rows, :] = _bf(kinv)
        binv_ref[rows, :] = _bf(binv)
        yield
        kt_ref[rows, :] = _bf(kinv * wc)
        bt_ref[rows, :] = _bf(-(binv * wc))
        vb_ref[rows, :] = _bf(v)
        for c in range(sub // chunk):
            wc_ref[r0 // chunk + c] = ends[c]
        bonus_ref[rows, :] = _headsum(r * kmod * rk_ref[...], ones_bd) * v
        gs_ref[rows, :] = _bf(_silu(g))
        yield

    parts = [part(r0) for r0 in range(0, tm, sub)]
    for _ in range(4):
        next(parts[0])
    for p in range(1, len(parts)):
        for _ in zip(parts[p - 1], parts[p]):
            pass
    for _ in parts[-1]:
        pass


def _odd_in_proj(h, seq, gain, mu, w_in, w0, w1, w2, a0, a1, a2, k_k, k_a, r_k, vres=None, v_first=None):
    t, d = h.shape
    tm = min(ODD_ROW_TILE, seq)
    chunk = SCAN_CHUNK
    tiles_per_seq = seq // tm
    sub_per_tile = tm // SUBLANES
    row = pl.BlockSpec((tm, d), lambda i: (i, 0))
    full = lambda a: pl.BlockSpec(a.shape, lambda i: (0,) * a.ndim)
    vec = lambda a: a.reshape(1, -1)
    args = [h, h, vec(gain), mu, w_in, vec(w0), w1, w2, vec(a0), a1, a2, vec(k_k), vec(k_a), vec(r_k)]
    specs = [row, pl.BlockSpec((SUBLANES, d), lambda i: (jnp.maximum(i * sub_per_tile - 1, 0), 0))]
    specs += [full(a) for a in args[2:]]
    if vres is not None:
        v0, v1, v2 = vres
        extra = [vec(v0), v1, v2]
        args += extra + [v_first]
        specs += [full(a) for a in extra] + [row]
    act = lambda dt: jax.ShapeDtypeStruct((t, d), dt)
    out_shape = [act(BF16)] * 7 + [jax.ShapeDtypeStruct((t // chunk, 1, d), F32), act(F32), act(BF16)]
    out_specs = [row] * 7 + [pl.BlockSpec((tm // chunk, 1, d), lambda i: (i, 0, 0)), row, row]
    if vres is None:
        out_shape.append(act(F32))
        out_specs.append(row)
    return pl.pallas_call(
        functools.partial(_odd_in_kernel, vres=vres is not None, tiles_per_seq=tiles_per_seq, chunk=chunk,
                          sub=min(ODD_SUB_ROWS, tm)),
        out_shape=out_shape,
        grid=(t // tm,),
        in_specs=specs,
        out_specs=out_specs,
        compiler_params=_params("parallel"),
        name="odd_in_proj",
    )(*args)


def _scan_kernel(rd_ref, kkd_ref, kinv_ref, binv_ref, kt_ref, bt_ref, v_ref, wc_ref, y_ref, state_ref, *, groups,
                 chunks):
    @pl.when(pl.program_id(0) == 0)
    def _():
        state_ref[...] = jnp.zeros_like(state_ref)

    batch = rd_ref.shape[0]
    c = rd_ref.shape[1] // chunks
    w = MXU_DIM
    hd = RWKV_HEAD_DIM
    row = lax.broadcasted_iota(jnp.int32, (c, w), 0)
    lane = lax.broadcasted_iota(jnp.int32, (c, w), 1)
    j = lane & (hd - 1)
    strict, incl, eye = j < row, j <= row, j == row
    blk16 = (j >> 4) == (row >> 4)
    blk32 = (j >> 5) == (row >> 5)
    head_of_lane = lane // hd
    head_masks = [head_of_lane == hh for hh in range(HEADS_PER_GROUP)]
    brow = lax.broadcasted_iota(jnp.int32, (w, w), 0)
    bcol = lax.broadcasted_iota(jnp.int32, (w, w), 1)
    bd = (brow // hd) == (bcol // hd)
    zero = jnp.zeros((), BF16)

    def stack(x):
        return jnp.concatenate([jnp.where(m, x, zero) for m in head_masks], axis=0)

    def pdot(x, y):
        return _dot(_bf(x), stack(_bf(y)))

    cols = [slice(g * w, (g + 1) * w) for g in range(groups)]
    rows = [slice(ch * c, (ch + 1) * c) for ch in range(chunks)]
    lanes = [(b, s) for b in range(batch) for s in cols]
    ix = [(b, r, s) for r in rows for b, s in lanes]
    ks = range(len(ix))
    rd = [rd_ref[i] for i in ix]
    kkd = [kkd_ref[i] for i in ix]
    v = [v_ref[i] for i in ix]
    lhs2 = [jnp.concatenate([kkd[k], rd[k]], axis=0) for k in ks]
    ak = [_dot_nt(lhs2[k], stack(kinv_ref[ix[k]])) for k in ks]
    ab = [_dot_nt(lhs2[k], stack(binv_ref[ix[k]])) for k in ks]
    a_kk = [_bf(jnp.where(strict, ak[k][:c], 0.0)) for k in ks]
    a_rk = [_bf(jnp.where(incl, ak[k][c:], 0.0)) for k in ks]
    n = [jnp.where(strict, ab[k][:c], 0.0) for k in ks]
    a_rb = [_bf(jnp.where(incl, ab[k][c:], 0.0)) for k in ks]

    nd = [jnp.where(blk16, n[k], 0.0) for k in ks]
    n2 = [pdot(nd[k], nd[k]) for k in ks]
    tinv = [jnp.where(eye, 1.0, 0.0) - nd[k] for k in ks]
    r1 = [pdot(jnp.concatenate([tinv[k], n2[k]], axis=0), n2[k]) for k in ks]
    tinv = [tinv[k] + r1[k][:c] for k in ks]
    r2 = [pdot(jnp.concatenate([tinv[k], r1[k][c:]], axis=0), r1[k][c:]) for k in ks]
    tinv = [tinv[k] + r2[k][:c] for k in ks]
    tinv = [tinv[k] + pdot(tinv[k], r2[k][c:]) for k in ks]
    off1 = [jnp.where(blk32 & jnp.logical_not(blk16), n[k], 0.0) for k in ks]
    m1 = [pdot(off1[k], tinv[k]) for k in ks]
    tinv = [tinv[k] - pdot(tinv[k], m1[k]) for k in ks]
    off2 = [jnp.where(blk32, 0.0, n[k]) for k in ks]
    m2 = [pdot(off2[k], tinv[k]) for k in ks]
    tinv = [_bf(tinv[k] - pdot(tinv[k], m2[k])) for k in ks]
    av = [_dot(jnp.concatenate([a_kk[k], a_rk[k]], axis=0), stack(v[k])) for k in ks]

    gs = range(len(lanes))
    state = [state_ref[g] for g in gs]
    for ch in range(chunks):
        chain = [ch * len(lanes) + g for g in gs]
        ps = [_dot_nt(lhs2[k], _bf(state[g])) for g, k in zip(gs, chain)]
        skb = [_bf(_dot(tinv[k], stack(_bf(ps[g][:c] + av[k][:c])))) for g, k in zip(gs, chain)]
        for g, k in zip(gs, chain):
            y_ref[ix[k]] = ps[g][c:] + av[k][c:] - _dot(a_rb[k], stack(skb[g]))
        upd = [_dot_tn(jnp.concatenate([v[k], skb[g]], axis=0),
                       jnp.concatenate([kt_ref[ix[k]], bt_ref[ix[k]]], axis=0))
               for g, k in zip(gs, chain)]
        state = [state[g] * wc_ref[lanes[g][0], ch, :, lanes[g][1]] + jnp.where(bd, upd[g], 0.0) for g in gs]
    for g in gs:
        state_ref[g] = state[g]


def _rwkv_scan(rd, kkd, kinv, binv, ktail, btail, vb, wc, batch, seq):
    t, d = rd.shape
    chunks = SCAN_CHUNKS_PER_STEP
    rows = SCAN_CHUNK * chunks
    groups = d // MXU_DIM
    blk = pl.BlockSpec((batch, rows, d), lambda n: (0, n, 0))
    seqs = [a.reshape(batch, seq, d) for a in (rd, kkd, kinv, binv, ktail, btail, vb)]
    y = pl.pallas_call(
        functools.partial(_scan_kernel, groups=groups, chunks=chunks),
        out_shape=jax.ShapeDtypeStruct((batch, seq, d), F32),
        grid=(seq // rows,),
        in_specs=[blk] * 7 + [pl.BlockSpec((batch, chunks, 1, d), lambda n: (0, n, 0, 0))],
        out_specs=blk,
        scratch_shapes=[pltpu.VMEM((batch * groups, MXU_DIM, MXU_DIM), F32)],
        compiler_params=_params("arbitrary"),
        name="rwkv7_scan",
    )(*seqs, wc.reshape(batch, seq // SCAN_CHUNK, 1, d))
    return y.reshape(t, d)


def kernel(x, p, norm_gain, final_gain, ple_proj, ple_gate, even_w_in, even_w_out, ret_gn_gain, odd_mu, odd_w_in, odd_w_out, rwkv_w0, rwkv_w1, rwkv_w2, rwkv_a0, rwkv_a1, rwkv_a2, rwkv_v0, rwkv_v1, rwkv_v2, rwkv_k_k, rwkv_k_a, rwkv_r_k, rwkv_lnx_gain, rwkv_lnx_bias):
    batch, seq, d = x.shape
    depth = p.shape[0]
    t = batch * seq
    h = x.reshape(t, d)
    p2 = p.reshape(depth, t, p.shape[-1])
    vec = lambda a: a.reshape(1, -1)
    v_first = None
    for i in range(depth):
        final = final_gain if i == depth - 1 else None
        w_gate, w_ple = _bf(ple_gate[i]), _bf(ple_proj[i])
        if i % 2 == 0:
            e = i // 2
            z = _even_in_proj(h, norm_gain[i], _bf(even_w_in[e]))
            mixed_a = _retention(z, batch, seq, ret_gn_gain[e])
            mixed_b = _stick_breaking(z, batch, seq)
            h = _out_call(_even_out_kernel, "even_out_proj_ple", h, [mixed_a, mixed_b], [_bf(even_w_out[e])],
                          p2, i, w_gate, w_ple, final)
        else:
            o = i // 2
            vres = None if v_first is None else (rwkv_v0[o - 1], _bf(rwkv_v1[o - 1]), _bf(rwkv_v2[o - 1]))
            outs = _odd_in_proj(
                h, seq, norm_gain[i], odd_mu[o], _bf(odd_w_in[o]), rwkv_w0[o], _bf(rwkv_w1[o]), _bf(rwkv_w2[o]),
                rwkv_a0[o], _bf(rwkv_a1[o]), _bf(rwkv_a2[o]), rwkv_k_k[o], rwkv_k_a[o], rwkv_r_k[o], vres, v_first)
            rd, kkd, kinv, binv, ktail, btail, vb, wc, bonus, gs = outs[:10]
            if v_first is None:
                v_first = outs[10]
            y = _rwkv_scan(rd, kkd, kinv, binv, ktail, btail, vb, wc, batch, seq)
            h = _out_call(_odd_out_kernel, "odd_out_proj_ple", h, [y, bonus, gs],
                          [vec(rwkv_lnx_gain[o]), vec(rwkv_lnx_bias[o]), _bf(odd_w_out[o])], p2, i, w_gate, w_ple,
                          final)
    return h.reshape(batch, seq, d)
```

```python
import functools
import math

import jax
import jax.numpy as jnp
import numpy as np
from jax import lax
from jax.experimental import pallas as pl
from jax.experimental.pallas import tpu as pltpu

F32 = jnp.float32
BF16 = jnp.bfloat16

LANES = 128
SUBLANES = 8
MXU_DIM = 256
VMEM_LIMIT_BYTES = 56 * 1024 * 1024

RET_HEADS = 4
RET_HEAD_DIM = 128
SB_HEADS = 4
SB_HEAD_DIM = 128
ROPE_BASE = 10000.0
RWKV_HEAD_DIM = 64
RMS_EPS = 1e-6
GN_EPS = 1e-5
LNX_EPS = 64e-5
F32_EXP_UNDERFLOW_LOG = -110.0
DECAY_SCALE = math.exp(-0.5)

ROW_TILE = 512
ODD_ROW_TILE = 512
ODD_SUB_ROWS = 256
IN_COL_TILE = 1024
RET_CHUNK = 256
SB_TILE = 256
SCAN_CHUNK = 64
SCAN_CHUNKS_PER_STEP = 4
HEADS_PER_GROUP = MXU_DIM // RWKV_HEAD_DIM


def _bf(x):
    return x.astype(BF16)


def _dot(a, b):
    return jnp.dot(a, b, preferred_element_type=F32)


def _dot_nt(a, b):
    return lax.dot_general(a, b, (((1,), (1,)), ((), ())), preferred_element_type=F32)


def _dot_tn(a, b):
    return lax.dot_general(a, b, (((0,), (0,)), ((), ())), preferred_element_type=F32)


def _split(x):
    hi = _bf(x)
    lo = _bf(x - hi.astype(F32))
    return hi, lo


def _rms(x, gain):
    return x * lax.rsqrt(jnp.mean(x * x, axis=-1, keepdims=True) + RMS_EPS) * gain


def _softplus(y):
    return jnp.maximum(y, 0.0) + jnp.log(1.0 + jnp.exp(-jnp.abs(y)))


def _silu(g):
    return g * jax.nn.sigmoid(g)


def _head_ones():
    r = lax.broadcasted_iota(jnp.int32, (MXU_DIM, MXU_DIM), 0)
    c = lax.broadcasted_iota(jnp.int32, (MXU_DIM, MXU_DIM), 1)
    return jnp.where((r // RWKV_HEAD_DIM) == (c // RWKV_HEAD_DIM), 1.0, 0.0).astype(BF16)


def _headsum(x, ones_bd):
    xb = _bf(x)
    cols = [_dot(xb[:, s * MXU_DIM:(s + 1) * MXU_DIM], ones_bd) for s in range(x.shape[1] // MXU_DIM)]
    return jnp.concatenate(cols, axis=1)


def _params(*sem):
    return pltpu.CompilerParams(dimension_semantics=sem, vmem_limit_bytes=VMEM_LIMIT_BYTES)


def _even_in_kernel(h_ref, g_ref, w_ref, z_ref):
    hn = _bf(_rms(h_ref[...], g_ref[...]))
    for j in range(w_ref.shape[1] // IN_COL_TILE):
        sl = slice(j * IN_COL_TILE, (j + 1) * IN_COL_TILE)
        z_ref[:, sl] = _bf(_dot(hn, w_ref[:, sl]))


def _even_in_proj(h, gain, w_in):
    t, d = h.shape
    f = w_in.shape[1]
    tm = min(ROW_TILE, t)
    return pl.pallas_call(
        _even_in_kernel,
        out_shape=jax.ShapeDtypeStruct((t, f), BF16),
        grid=(t // tm,),
        in_specs=[
            pl.BlockSpec((tm, d), lambda i: (i, 0)),
            pl.BlockSpec((1, d), lambda i: (0, 0)),
            pl.BlockSpec((d, f), lambda i: (0, 0)),
        ],
        out_specs=pl.BlockSpec((tm, f), lambda i: (i, 0)),
        compiler_params=_params("parallel"),
        name="even_in_proj",
    )(h, gain.reshape(1, d), w_in)


def _ret_kernel(q_ref, k_ref, v_ref, g_ref, cos_ref, sin_ref, intra_ref, qdec_ref, kdec_ref, cdec_ref,
                gn_ref, o_ref, state_ref):
    @pl.when(pl.program_id(1) == 0)
    def _():
        state_ref[...] = jnp.zeros_like(state_ref)

    hd = RET_HEAD_DIM
    half = hd // 2
    cos = cos_ref[...]
    sin = sin_ref[...]
    hs = range(RET_HEADS)
    sl = [slice(h * hd, (h + 1) * hd) for h in hs]
    q = [q_ref[:, s].astype(F32) for s in sl]
    k = [k_ref[:, s].astype(F32) for s in sl]
    qr = [q[h] * cos + pltpu.roll(q[h], half, 1) * sin for h in hs]
    kr = [(k[h] * cos + pltpu.roll(k[h], half, 1) * sin) * (hd ** -0.5) for h in hs]
    v = [v_ref[:, s] for s in sl]
    scores = [_dot_nt(_bf(qr[h]), _bf(kr[h])) * intra_ref[h] for h in hs]
    state = [state_ref[h] for h in hs]
    out = [_dot(_bf(scores[h]), v[h]) + _dot(_bf(qr[h] * qdec_ref[h]), _bf(state[h])) for h in hs]
    for h in hs:
        state_ref[h] = state[h] * cdec_ref[h, 0:1, :] + _dot_tn(_bf(kr[h] * kdec_ref[h]), v[h])
    for h in hs:
        mu = jnp.mean(out[h], axis=-1, keepdims=True)
        dev = out[h] - mu
        var = jnp.mean(dev * dev, axis=-1, keepdims=True)
        y = dev * lax.rsqrt(var + GN_EPS) * gn_ref[:, sl[h]]
        o_ref[:, sl[h]] = _bf(y * _silu(g_ref[:, sl[h]].astype(F32)))


def _retention(z, batch, seq, gn_gain):
    t = z.shape[0]
    c = min(RET_CHUNK, seq)
    nc = seq // c
    hd = RET_HEAD_DIM
    half = hd // 2
    f32 = np.float32
    inv_freq = f32(ROPE_BASE) ** (-np.arange(half, dtype=f32) / f32(half))
    ang = np.arange(seq, dtype=f32)[:, None] * inv_freq[None, :]
    cos2 = np.concatenate([np.cos(ang), np.cos(ang)], axis=-1)
    sin2 = np.concatenate([-np.sin(ang), np.sin(ang)], axis=-1)
    lg = np.log(f32(1.0) - f32(2.0) ** (f32(-5.0) - np.arange(RET_HEADS, dtype=f32)))
    ci = np.arange(c, dtype=f32)
    diff = ci[:, None] - ci[None, :]
    intra = np.where(diff[None] >= 0, np.exp(np.maximum(diff, f32(0.0))[None] * lg[:, None, None]), f32(0.0))
    qdec = np.ascontiguousarray(np.broadcast_to(np.exp((ci + 1)[None, :, None] * lg[:, None, None]),
                                                (RET_HEADS, c, hd)))
    kdec = np.ascontiguousarray(np.broadcast_to(np.exp((c - 1 - ci)[None, :, None] * lg[:, None, None]),
                                                (RET_HEADS, c, hd)))
    cdec = np.ascontiguousarray(np.broadcast_to(np.exp(f32(c) * lg)[:, None, None], (RET_HEADS, SUBLANES, hd)))
    cos2, sin2, intra, qdec, kdec, cdec = (jnp.asarray(a, F32) for a in (cos2, sin2, intra, qdec, kdec, cdec))

    width = RET_HEADS * hd

    def zcol(j):
        return pl.BlockSpec((c, width), lambda b, n: (b * nc + n, j))

    full = lambda a: pl.BlockSpec(a.shape, lambda b, n: (0,) * a.ndim)
    gn = gn_gain.reshape(1, -1)
    return pl.pallas_call(
        _ret_kernel,
        out_shape=jax.ShapeDtypeStruct((t, width), BF16),
        grid=(batch, nc),
        in_specs=[
            zcol(0), zcol(1), zcol(2), zcol(3),
            pl.BlockSpec((c, hd), lambda b, n: (n, 0)),
            pl.BlockSpec((c, hd), lambda b, n: (n, 0)),
            full(intra), full(qdec), full(kdec), full(cdec), full(gn),
        ],
        out_specs=pl.BlockSpec((c, width), lambda b, n: (b * nc + n, 0)),
        scratch_shapes=[pltpu.VMEM((RET_HEADS, hd, hd), F32)],
        compiler_params=_params("parallel", "arbitrary"),
        name="retention",
    )(z, z, z, z, cos2, sin2, intra, qdec, kdec, cdec, gn)


def _sb_kernel(q_ref, k_ref, v_ref, g_ref, o_ref, acc_ref, run_ref, *, tile):
    i = pl.program_id(1)
    hd = SB_HEAD_DIM
    scale = hd ** -0.5
    hs = range(SB_HEADS)
    cols = [slice(h * hd, (h + 1) * hd) for h in hs]
    q = [q_ref[:, s] for s in cols]
    trow = lax.broadcasted_iota(jnp.int32, (tile, tile), 0)
    tcol = lax.broadcasted_iota(jnp.int32, (tile, tile), 1)
    tri = jnp.where(trow >= tcol, 1.0, 0.0).astype(BF16)
    causal = tcol < trow

    acc_ref[...] = jnp.zeros_like(acc_ref)
    run_ref[...] = jnp.zeros_like(run_ref)

    def span(kb, diagonal):
        ks = pl.multiple_of(kb * tile, tile)
        z = [_dot_nt(q[h], k_ref[pl.ds(ks, tile), cols[h]]) * scale for h in hs]
        l1mb = [-_softplus(z[h]) for h in hs]
        if diagonal:
            l1mb = [jnp.where(causal, l1mb[h], 0.0) for h in hs]
        parts = [_split(l1mb[h]) for h in hs]
        both = [_dot(jnp.concatenate(parts[h], axis=0), tri) for h in hs]
        cs = [both[h][:tile] + both[h][tile:] for h in hs]
        run = [run_ref[h] for h in hs]
        w = [jnp.exp(z[h] + cs[h] + jnp.concatenate([run[h]] * (tile // LANES), axis=1)) for h in hs]
        if diagonal:
            w = [jnp.where(causal, w[h], 0.0) for h in hs]
        for h in hs:
            acc_ref[h] += _dot(_bf(w[h]), v_ref[pl.ds(ks, tile), cols[h]])
            run_ref[h] = run[h] + jnp.broadcast_to(cs[h][:, 0:1], run[h].shape)

    def alive():
        return (jnp.max(run_ref[...]) >= F32_EXP_UNDERFLOW_LOG).astype(jnp.int32)

    span(i, True)

    def body(carry):
        j, _ = carry
        span(i - 1 - j, False)
        return j + 1, alive()

    lax.while_loop(lambda c: (c[0] < i) & (c[1] > 0), body, (jnp.int32(0), alive()))
    for h in hs:
        o_ref[:, cols[h]] = _bf(acc_ref[h] * _silu(g_ref[:, cols[h]].astype(F32)))


def _stick_breaking(z, batch, seq):
    t = z.shape[0]
    width = SB_HEADS * SB_HEAD_DIM
    tile = min(SB_TILE, seq)
    nq = seq // tile
    base = 4
    return pl.pallas_call(
        functools.partial(_sb_kernel, tile=tile),
        out_shape=jax.ShapeDtypeStruct((t, width), BF16),
        grid=(batch, nq),
        in_specs=[
            pl.BlockSpec((tile, width), lambda b, i: (b * nq + i, base)),
            pl.BlockSpec((seq, width), lambda b, i: (b, base + 1)),
            pl.BlockSpec((seq, width), lambda b, i: (b, base + 2)),
            pl.BlockSpec((tile, width), lambda b, i: (b * nq + i, base + 3)),
        ],
        out_specs=pl.BlockSpec((tile, width), lambda b, i: (b * nq + i, 0)),
        scratch_shapes=[pltpu.VMEM((SB_HEADS, tile, SB_HEAD_DIM), F32), pltpu.VMEM((SB_HEADS, tile, LANES), F32)],
        compiler_params=_params("parallel", "arbitrary"),
        name="stick_breaking",
    )(z, z, z, z)


def _ple_tail(h1, p_ref, wg_ref, wp_ref, rest, final):
    gate = jax.nn.sigmoid(_dot(_bf(h1), wg_ref[...]))
    h2 = h1 + gate * _dot(_bf(p_ref[...]), wp_ref[...])
    if final:
        fg_ref, o_ref = rest
        o_ref[...] = _rms(h2, fg_ref[...])
    else:
        (o_ref,) = rest
        o_ref[...] = h2


def _even_out_kernel(h_ref, ma_ref, mb_ref, w_ref, p_ref, wg_ref, wp_ref, *rest, final):
    wa = ma_ref.shape[1]
    h1 = h_ref[...] + _dot(ma_ref[...], w_ref[:wa, :]) + _dot(mb_ref[...], w_ref[wa:, :])
    _ple_tail(h1, p_ref, wg_ref, wp_ref, rest, final)


def _odd_out_kernel(h_ref, y_ref, bonus_ref, gs_ref, lg_ref, lb_ref, w_ref, p_ref, wg_ref, wp_ref, *rest, final):
    ones_bd = _head_ones()
    inv_n = 1.0 / RWKV_HEAD_DIM
    y = y_ref[...]
    dev = y - _headsum(y, ones_bd) * inv_n
    var = _headsum(dev * dev, ones_bd) * inv_n
    yn = dev * lax.rsqrt(var + LNX_EPS) * lg_ref[...] + lb_ref[...]
    mixed = _bf((yn + bonus_ref[...]) * gs_ref[...].astype(F32))
    _ple_tail(h_ref[...] + _dot(mixed, w_ref[...]), p_ref, wg_ref, wp_ref, rest, final)


def _out_call(kern, name, h, row_args, full_args_mid, p, layer, w_gate, w_ple, final_gain):
    t, d = h.shape
    tm = min(ROW_TILE, t)
    final = final_gain is not None
    row = lambda a: pl.BlockSpec((tm, a.shape[1]), lambda i: (i, 0))
    full = lambda a: pl.BlockSpec(a.shape, lambda i: (0,) * a.ndim)
    args = [h, *row_args, *full_args_mid, p, w_gate, w_ple]
    p_spec = pl.BlockSpec((None, tm, p.shape[2]), lambda i: (layer, i, 0))
    specs = [row(h)] + [row(a) for a in row_args] + [full(a) for a in full_args_mid] + [p_spec, full(w_gate),
                                                                                        full(w_ple)]
    if final:
        args.append(final_gain.reshape(1, d))
        specs.append(full(args[-1]))
    return pl.pallas_call(
        functools.partial(kern, final=final),
        out_shape=jax.ShapeDtypeStruct((t, d), F32),
        grid=(t // tm,),
        in_specs=specs,
        out_specs=row(h),
        compiler_params=_params("parallel"),
        name=name,
    )(*args)


def _odd_in_kernel(*refs, vres, tiles_per_seq, chunk, sub):
    (h_ref, hp_ref, gain_ref, mu_ref, win_ref, w0_ref, w1_ref, w2_ref, a0_ref, a1_ref, a2_ref,
     kk_ref, ka_ref, rk_ref) = refs[:14]
    if vres:
        v0_ref, v1_ref, v2_ref, vf_ref = refs[14:18]
        outs = refs[18:]
    else:
        outs = refs[14:]
    rd_ref, kkd_ref, kinv_ref, binv_ref, kt_ref, bt_ref, vb_ref, wc_ref, bonus_ref, gs_ref = outs[:10]

    gain = gain_ref[...]
    hn_all = _rms(h_ref[...], gain)
    hnp = _rms(hp_ref[...], gain)
    tm = hn_all.shape[0]
    first = (pl.program_id(0) % tiles_per_seq) == 0
    prev_row = jnp.where(first, 0.0, hnp[SUBLANES - 1:SUBLANES, :])
    rowid = lax.broadcasted_iota(jnp.int32, hn_all.shape, 0)
    xx_all = jnp.where(rowid == 0, prev_row, pltpu.roll(hn_all, 1, 0)) - hn_all
    ones_bd = _head_ones()
    ka = ka_ref[...]
    ka_rest = 1.0 - ka
    ri = lax.broadcasted_iota(jnp.int32, (sub, sub), 0)
    ci = lax.broadcasted_iota(jnp.int32, (sub, sub), 1)
    sel = jnp.where(((ri // chunk) == (ci // chunk)) & (ri >= ci), 1.0, 0.0).astype(BF16)

    def part(r0):
        rows = slice(r0, r0 + sub)
        hn, xx = hn_all[rows], xx_all[rows]
        mix = lambda p: _bf(hn + xx * mu_ref[p:p + 1, :])
        r = _dot(mix(0), win_ref[0])
        yield
        k = _dot(mix(1), win_ref[1])
        yield
        xv = mix(2)
        v = _dot(xv, win_ref[2])
        yield
        g = _dot(mix(3), win_ref[3])
        yield
        xw = mix(4)
        xa = mix(5)
        w_in = w0_ref[...] + _dot(_bf(jnp.tanh(_dot(xw, w1_ref[...]))), w2_ref[...])
        lw = -DECAY_SCALE * jax.nn.sigmoid(w_in)
        a = jax.nn.sigmoid(a0_ref[...] + _dot(_bf(_dot(xa, a1_ref[...])), a2_ref[...]))
        if vres:
            mixv = jax.nn.sigmoid(v0_ref[...] + _dot(_bf(_dot(xv, v1_ref[...])), v2_ref[...]))
            v = v + (vf_ref[rows, :] - v) * mixv
        else:
            outs[10][rows, :] = v
        yield
        kkraw = k * kk_ref[...]
        kk = kkraw * lax.rsqrt(jnp.maximum(_headsum(kkraw * kkraw, ones_bd), 1e-24))
        kmod = k * (a * ka + ka_rest)
        b = kk * a
        hi, lo = _split(lw)
        cum = _dot(sel, hi) + _dot(sel, lo)
        ends = [jnp.exp(cum[(c + 1) * chunk - 1:(c + 1) * chunk, :]) for c in range(sub // chunk)]
        spread = lambda es: jnp.concatenate([jnp.broadcast_to(e, (chunk, e.shape[1])) for e in es], axis=0)
        wc, wc_neg = spread(ends), spread([-e for e in ends])
        yield
        w_inv = jnp.exp(-cum)
        kinv = kmod * w_inv
        binv = b * w_inv
        rd_ref[rows, :] = _bf(r * jnp.exp(cum))
        kkd_ref[rows, :] = _bf(kk * jnp.exp(cum - lw))
        kinv_ref[rows, :] = _bf(kinv)
        binv_ref[rows, :] = _bf(binv)
        yield
        kt_ref[rows, :] = _bf(kinv * wc)
        bt_ref[rows, :] = _bf(binv * wc_neg)
        vb_ref[rows, :] = _bf(v)
        for c in range(sub // chunk):
            wc_ref[r0 // chunk + c] = ends[c]
        bonus_ref[rows, :] = _headsum(r * kmod * rk_ref[...], ones_bd) * v
        gs_ref[rows, :] = _bf(_silu(g))
        yield

    parts = [part(r0) for r0 in range(0, tm, sub)]
    for _ in range(4):
        next(parts[0])
    for p in range(1, len(parts)):
        for _ in zip(parts[p - 1], parts[p]):
            pass
    for _ in parts[-1]:
        pass


def _odd_in_proj(h, seq, gain, mu, w_in, w0, w1, w2, a0, a1, a2, k_k, k_a, r_k, vres=None, v_first=None):
    t, d = h.shape
    tm = min(ODD_ROW_TILE, seq)
    chunk = SCAN_CHUNK
    tiles_per_seq = seq // tm
    sub_per_tile = tm // SUBLANES
    row = pl.BlockSpec((tm, d), lambda i: (i, 0))
    full = lambda a: pl.BlockSpec(a.shape, lambda i: (0,) * a.ndim)
    vec = lambda a: a.reshape(1, -1)
    args = [h, h, vec(gain), mu, w_in, vec(w0), w1, w2, vec(a0), a1, a2, vec(k_k), vec(k_a), vec(r_k)]
    specs = [row, pl.BlockSpec((SUBLANES, d), lambda i: (jnp.maximum(i * sub_per_tile - 1, 0), 0))]
    specs += [full(a) for a in args[2:]]
    if vres is not None:
        v0, v1, v2 = vres
        extra = [vec(v0), v1, v2]
        args += extra + [v_first]
        specs += [full(a) for a in extra] + [row]
    act = lambda dt: jax.ShapeDtypeStruct((t, d), dt)
    out_shape = [act(BF16)] * 7 + [jax.ShapeDtypeStruct((t // chunk, 1, d), F32), act(F32), act(BF16)]
    out_specs = [row] * 7 + [pl.BlockSpec((tm // chunk, 1, d), lambda i: (i, 0, 0)), row, row]
    if vres is None:
        out_shape.append(act(F32))
        out_specs.append(row)
    return pl.pallas_call(
        functools.partial(_odd_in_kernel, vres=vres is not None, tiles_per_seq=tiles_per_seq, chunk=chunk,
                          sub=min(ODD_SUB_ROWS, tm)),
        out_shape=out_shape,
        grid=(t // tm,),
        in_specs=specs,
        out_specs=out_specs,
        compiler_params=_params("parallel"),
        name="odd_in_proj",
    )(*args)


def _scan_kernel(rd_ref, kkd_ref, kinv_ref, binv_ref, kt_ref, bt_ref, v_ref, wc_ref, y_ref, state_ref, *, groups,
                 chunks):
    @pl.when(pl.program_id(0) == 0)
    def _():
        state_ref[...] = jnp.zeros_like(state_ref)

    batch = rd_ref.shape[0]
    c = rd_ref.shape[1] // chunks
    w = MXU_DIM
    hd = RWKV_HEAD_DIM
    row = lax.broadcasted_iota(jnp.int32, (c, w), 0)
    lane = lax.broadcasted_iota(jnp.int32, (c, w), 1)
    j = lane & (hd - 1)
    strict, incl, eye = j < row, j <= row, j == row
    blk16 = (j >> 4) == (row >> 4)
    blk32 = (j >> 5) == (row >> 5)
    head_of_lane = lane // hd
    head_masks = [head_of_lane == hh for hh in range(HEADS_PER_GROUP)]
    brow = lax.broadcasted_iota(jnp.int32, (w, w), 0)
    bcol = lax.broadcasted_iota(jnp.int32, (w, w), 1)
    bd = (brow // hd) == (bcol // hd)
    zero = jnp.zeros((), BF16)

    def stack(x):
        return jnp.concatenate([jnp.where(m, x, zero) for m in head_masks], axis=0)

    def pdot(x, y):
        return _dot(_bf(x), stack(_bf(y)))

    cols = [slice(g * w, (g + 1) * w) for g in range(groups)]
    rows = [slice(ch * c, (ch + 1) * c) for ch in range(chunks)]
    lanes = [(b, s) for b in range(batch) for s in cols]
    ix = [(b, r, s) for r in rows for b, s in lanes]
    ks = range(len(ix))
    rd = [rd_ref[i] for i in ix]
    kkd = [kkd_ref[i] for i in ix]
    v = [v_ref[i] for i in ix]
    lhs2 = [jnp.concatenate([kkd[k], rd[k]], axis=0) for k in ks]
    ak = [_dot_nt(lhs2[k], stack(kinv_ref[ix[k]])) for k in ks]
    ab = [_dot_nt(lhs2[k], stack(binv_ref[ix[k]])) for k in ks]
    a_kk = [_bf(jnp.where(strict, ak[k][:c], 0.0)) for k in ks]
    a_rk = [_bf(jnp.where(incl, ak[k][c:], 0.0)) for k in ks]
    n = [jnp.where(strict, ab[k][:c], 0.0) for k in ks]
    a_rb = [_bf(jnp.where(incl, ab[k][c:], 0.0)) for k in ks]

    nd = [jnp.where(blk16, n[k], 0.0) for k in ks]
    n2 = [pdot(nd[k], nd[k]) for k in ks]
    tinv = [jnp.where(eye, 1.0, 0.0) - nd[k] for k in ks]
    r1 = [pdot(jnp.concatenate([tinv[k], n2[k]], axis=0), n2[k]) for k in ks]
    tinv = [tinv[k] + r1[k][:c] for k in ks]
    r2 = [pdot(jnp.concatenate([tinv[k], r1[k][c:]], axis=0), r1[k][c:]) for k in ks]
    tinv = [tinv[k] + r2[k][:c] for k in ks]
    tinv = [tinv[k] + pdot(tinv[k], r2[k][c:]) for k in ks]
    off1 = [jnp.where(blk32 & jnp.logical_not(blk16), n[k], 0.0) for k in ks]
    m1 = [pdot(off1[k], tinv[k]) for k in ks]
    tinv = [tinv[k] - pdot(tinv[k], m1[k]) for k in ks]
    off2 = [jnp.where(blk32, 0.0, n[k]) for k in ks]
    m2 = [pdot(off2[k], tinv[k]) for k in ks]
    tinv = [_bf(tinv[k] - pdot(tinv[k], m2[k])) for k in ks]
    av = [_dot(jnp.concatenate([a_kk[k], a_rk[k]], axis=0), stack(v[k])) for k in ks]

    gs = range(len(lanes))
    state = [state_ref[g] for g in gs]
    for ch in range(chunks):
        chain = [ch * len(lanes) + g for g in gs]
        ps = [_dot_nt(lhs2[k], _bf(state[g])) for g, k in zip(gs, chain)]
        skb = [_bf(_dot(tinv[k], stack(_bf(ps[g][:c] + av[k][:c])))) for g, k in zip(gs, chain)]
        for g, k in zip(gs, chain):
            y_ref[ix[k]] = ps[g][c:] + av[k][c:] - _dot(a_rb[k], stack(skb[g]))
        upd = [_dot_tn(jnp.concatenate([v[k], skb[g]], axis=0),
                       jnp.concatenate([kt_ref[ix[k]], bt_ref[ix[k]]], axis=0))
               for g, k in zip(gs, chain)]
        state = [state[g] * wc_ref[lanes[g][0], ch, :, lanes[g][1]] + jnp.where(bd, upd[g], 0.0) for g in gs]
    for g in gs:
        state_ref[g] = state[g]


def _rwkv_scan(rd, kkd, kinv, binv, ktail, btail, vb, wc, batch, seq):
    t, d = rd.shape
    chunks = SCAN_CHUNKS_PER_STEP
    rows = SCAN_CHUNK * chunks
    groups = d // MXU_DIM
    blk = pl.BlockSpec((batch, rows, d), lambda n: (0, n, 0))
    seqs = [a.reshape(batch, seq, d) for a in (rd, kkd, kinv, binv, ktail, btail, vb)]
    y = pl.pallas_call(
        functools.partial(_scan_kernel, groups=groups, chunks=chunks),
        out_shape=jax.ShapeDtypeStruct((batch, seq, d), F32),
        grid=(seq // rows,),
        in_specs=[blk] * 7 + [pl.BlockSpec((batch, chunks, 1, d), lambda n: (0, n, 0, 0))],
        out_specs=blk,
        scratch_shapes=[pltpu.VMEM((batch * groups, MXU_DIM, MXU_DIM), F32)],
        compiler_params=_params("arbitrary"),
        name="rwkv7_scan",
    )(*seqs, wc.reshape(batch, seq // SCAN_CHUNK, 1, d))
    return y.reshape(t, d)


def kernel(x, p, norm_gain, final_gain, ple_proj, ple_gate, even_w_in, even_w_out, ret_gn_gain, odd_mu, odd_w_in, odd_w_out, rwkv_w0, rwkv_w1, rwkv_w2, rwkv_a0, rwkv_a1, rwkv_a2, rwkv_v0, rwkv_v1, rwkv_v2, rwkv_k_k, rwkv_k_a, rwkv_r_k, rwkv_lnx_gain, rwkv_lnx_bias):
    batch, seq, d = x.shape
    depth = p.shape[0]
    t = batch * seq
    h = x.reshape(t, d)
    p2 = p.reshape(depth, t, p.shape[-1])
    vec = lambda a: a.reshape(1, -1)
    v_first = None
    for i in range(depth):
        final = final_gain if i == depth - 1 else None
        w_gate, w_ple = _bf(ple_gate[i]), _bf(ple_proj[i])
        if i % 2 == 0:
            e = i // 2
            z = _even_in_proj(h, norm_gain[i], _bf(even_w_in[e]))
            mixed_a = _retention(z, batch, seq, ret_gn_gain[e])
            mixed_b = _stick_breaking(z, batch, seq)
            h = _out_call(_even_out_kernel, "even_out_proj_ple", h, [mixed_a, mixed_b], [_bf(even_w_out[e])],
                          p2, i, w_gate, w_ple, final)
        else:
            o = i // 2
            vres = None if v_first is None else (rwkv_v0[o - 1], _bf(rwkv_v1[o - 1]), _bf(rwkv_v2[o - 1]))
            outs = _odd_in_proj(
                h, seq, norm_gain[i], odd_mu[o], _bf(odd_w_in[o]), rwkv_w0[o], _bf(rwkv_w1[o]), _bf(rwkv_w2[o]),
                rwkv_a0[o], _bf(rwkv_a1[o]), _bf(rwkv_a2[o]), rwkv_k_k[o], rwkv_k_a[o], rwkv_r_k[o], vres, v_first)
            rd, kkd, kinv, binv, ktail, btail, vb, wc, bonus, gs = outs[:10]
            if v_first is None:
                v_first = outs[10]
            y = _rwkv_scan(rd, kkd, kinv, binv, ktail, btail, vb, wc, batch, seq)
            h = _out_call(_odd_out_kernel, "odd_out_proj_ple", h, [y, bonus, gs],
                          [vec(rwkv_lnx_gain[o]), vec(rwkv_lnx_bias[o]), _bf(odd_w_out[o])], p2, i, w_gate, w_ple,
                          final)
    return h.reshape(batch, seq, d)
```

```python
import functools
import math

import jax
import jax.numpy as jnp
import numpy as np
from jax import lax
from jax.experimental import pallas as pl
from jax.experimental.pallas import tpu as pltpu

F32 = jnp.float32
BF16 = jnp.bfloat16

LANES = 128
SUBLANES = 8
MXU_DIM = 256
VMEM_LIMIT_BYTES = 56 * 1024 * 1024

RET_HEADS = 4
RET_HEAD_DIM = 128
SB_HEADS = 4
SB_HEAD_DIM = 128
ROPE_BASE = 10000.0
RWKV_HEAD_DIM = 64
RMS_EPS = 1e-6
GN_EPS = 1e-5
LNX_EPS = 64e-5
F32_EXP_UNDERFLOW_LOG = -110.0
DECAY_SCALE = math.exp(-0.5)
LOG2E = math.log2(math.e)

ROW_TILE = 512
ODD_ROW_TILE = 512
ODD_SUB_ROWS = 256
IN_COL_TILE = 1024
RET_CHUNK = 256
SB_TILE = 256
SCAN_CHUNK = 64
SCAN_CHUNKS_PER_STEP = 4
HEADS_PER_GROUP = MXU_DIM // RWKV_HEAD_DIM


def _bf(x):
    return x.astype(BF16)


def _dot(a, b):
    return jnp.dot(a, b, preferred_element_type=F32)


def _dot_nt(a, b):
    return lax.dot_general(a, b, (((1,), (1,)), ((), ())), preferred_element_type=F32)


def _dot_tn(a, b):
    return lax.dot_general(a, b, (((0,), (0,)), ((), ())), preferred_element_type=F32)


def _split(x):
    hi = _bf(x)
    lo = _bf(x - hi.astype(F32))
    return hi, lo


def _rms(x, gain):
    return x * lax.rsqrt(jnp.mean(x * x, axis=-1, keepdims=True) + RMS_EPS) * gain


def _silu(g):
    return g * jax.nn.sigmoid(g)


def _head_ones():
    r = lax.broadcasted_iota(jnp.int32, (MXU_DIM, MXU_DIM), 0)
    c = lax.broadcasted_iota(jnp.int32, (MXU_DIM, MXU_DIM), 1)
    return jnp.where((r // RWKV_HEAD_DIM) == (c // RWKV_HEAD_DIM), 1.0, 0.0).astype(BF16)


def _headsum(x, ones_bd):
    xb = _bf(x)
    cols = [_dot(xb[:, s * MXU_DIM:(s + 1) * MXU_DIM], ones_bd) for s in range(x.shape[1] // MXU_DIM)]
    return jnp.concatenate(cols, axis=1)


def _params(*sem):
    return pltpu.CompilerParams(dimension_semantics=sem, vmem_limit_bytes=VMEM_LIMIT_BYTES)


def _even_in_kernel(h_ref, g_ref, w_ref, z_ref):
    hn = _bf(_rms(h_ref[...], g_ref[...]))
    for j in range(w_ref.shape[1] // IN_COL_TILE):
        sl = slice(j * IN_COL_TILE, (j + 1) * IN_COL_TILE)
        z_ref[:, sl] = _bf(_dot(hn, w_ref[:, sl]))


def _even_in_proj(h, gain, w_in):
    t, d = h.shape
    f = w_in.shape[1]
    tm = min(ROW_TILE, t)
    return pl.pallas_call(
        _even_in_kernel,
        out_shape=jax.ShapeDtypeStruct((t, f), BF16),
        grid=(t // tm,),
        in_specs=[
            pl.BlockSpec((tm, d), lambda i: (i, 0)),
            pl.BlockSpec((1, d), lambda i: (0, 0)),
            pl.BlockSpec((d, f), lambda i: (0, 0)),
        ],
        out_specs=pl.BlockSpec((tm, f), lambda i: (i, 0)),
        compiler_params=_params("parallel"),
        name="even_in_proj",
    )(h, gain.reshape(1, d), w_in)


def _ret_kernel(q_ref, k_ref, v_ref, g_ref, cos_ref, sin_ref, intra_ref, qdec_ref, kdec_ref, cdec_ref,
                gn_ref, o_ref, state_ref):
    @pl.when(pl.program_id(1) == 0)
    def _():
        state_ref[...] = jnp.zeros_like(state_ref)

    hd = RET_HEAD_DIM
    half = hd // 2
    cos = cos_ref[...]
    sin = sin_ref[...]
    hs = range(RET_HEADS)
    sl = [slice(h * hd, (h + 1) * hd) for h in hs]
    q = [q_ref[:, s].astype(F32) for s in sl]
    k = [k_ref[:, s].astype(F32) for s in sl]
    qr = [q[h] * cos + pltpu.roll(q[h], half, 1) * sin for h in hs]
    kr = [(k[h] * cos + pltpu.roll(k[h], half, 1) * sin) * (hd ** -0.5) for h in hs]
    v = [v_ref[:, s] for s in sl]
    scores = [_dot_nt(_bf(qr[h]), _bf(kr[h])) * intra_ref[h] for h in hs]
    state = [state_ref[h] for h in hs]
    out = [_dot(_bf(scores[h]), v[h]) + _dot(_bf(qr[h] * qdec_ref[h]), _bf(state[h])) for h in hs]
    for h in hs:
        state_ref[h] = state[h] * cdec_ref[h, 0:1, :] + _dot_tn(_bf(kr[h] * kdec_ref[h]), v[h])
    for h in hs:
        mu = jnp.mean(out[h], axis=-1, keepdims=True)
        dev = out[h] - mu
        var = jnp.mean(dev * dev, axis=-1, keepdims=True)
        y = dev * lax.rsqrt(var + GN_EPS) * gn_ref[:, sl[h]]
        o_ref[:, sl[h]] = _bf(y * _silu(g_ref[:, sl[h]].astype(F32)))


def _retention(z, batch, seq, gn_gain):
    t = z.shape[0]
    c = min(RET_CHUNK, seq)
    nc = seq // c
    hd = RET_HEAD_DIM
    half = hd // 2
    f32 = np.float32
    inv_freq = f32(ROPE_BASE) ** (-np.arange(half, dtype=f32) / f32(half))
    ang = np.arange(seq, dtype=f32)[:, None] * inv_freq[None, :]
    cos2 = np.concatenate([np.cos(ang), np.cos(ang)], axis=-1)
    sin2 = np.concatenate([-np.sin(ang), np.sin(ang)], axis=-1)
    lg = np.log(f32(1.0) - f32(2.0) ** (f32(-5.0) - np.arange(RET_HEADS, dtype=f32)))
    ci = np.arange(c, dtype=f32)
    diff = ci[:, None] - ci[None, :]
    intra = np.where(diff[None] >= 0, np.exp(np.maximum(diff, f32(0.0))[None] * lg[:, None, None]), f32(0.0))
    qdec = np.ascontiguousarray(np.broadcast_to(np.exp((ci + 1)[None, :, None] * lg[:, None, None]),
                                                (RET_HEADS, c, hd)))
    kdec = np.ascontiguousarray(np.broadcast_to(np.exp((c - 1 - ci)[None, :, None] * lg[:, None, None]),
                                                (RET_HEADS, c, hd)))
    cdec = np.ascontiguousarray(np.broadcast_to(np.exp(f32(c) * lg)[:, None, None], (RET_HEADS, SUBLANES, hd)))
    cos2, sin2, intra, qdec, kdec, cdec = (jnp.asarray(a, F32) for a in (cos2, sin2, intra, qdec, kdec, cdec))

    width = RET_HEADS * hd

    def zcol(j):
        return pl.BlockSpec((c, width), lambda b, n: (b * nc + n, j))

    full = lambda a: pl.BlockSpec(a.shape, lambda b, n: (0,) * a.ndim)
    gn = gn_gain.reshape(1, -1)
    return pl.pallas_call(
        _ret_kernel,
        out_shape=jax.ShapeDtypeStruct((t, width), BF16),
        grid=(batch, nc),
        in_specs=[
            zcol(0), zcol(1), zcol(2), zcol(3),
            pl.BlockSpec((c, hd), lambda b, n: (n, 0)),
            pl.BlockSpec((c, hd), lambda b, n: (n, 0)),
            full(intra), full(qdec), full(kdec), full(cdec), full(gn),
        ],
        out_specs=pl.BlockSpec((c, width), lambda b, n: (b * nc + n, 0)),
        scratch_shapes=[pltpu.VMEM((RET_HEADS, hd, hd), F32)],
        compiler_params=_params("parallel", "arbitrary"),
        name="retention",
    )(z, z, z, z, cos2, sin2, intra, qdec, kdec, cdec, gn)


def _sb_kernel(q_ref, k_ref, v_ref, g_ref, o_ref, acc_ref, run_ref, *, tile):
    i = pl.program_id(1)
    hd = SB_HEAD_DIM
    scale = hd ** -0.5
    hs = range(SB_HEADS)
    cols = [slice(h * hd, (h + 1) * hd) for h in hs]
    q = [q_ref[:, s] for s in cols]
    trow = lax.broadcasted_iota(jnp.int32, (tile, tile), 0)
    tcol = lax.broadcasted_iota(jnp.int32, (tile, tile), 1)
    tri = jnp.where(trow >= tcol, 1.0, 0.0).astype(BF16)
    causal = tcol < trow

    acc_ref[...] = jnp.zeros_like(acc_ref)
    run_ref[...] = jnp.zeros_like(run_ref)

    def span(kb, diagonal):
        ks = pl.multiple_of(kb * tile, tile)
        nz = [_dot_nt(q[h], k_ref[pl.ds(ks, tile), cols[h]]) * (-scale) for h in hs]
        l1mb = [jnp.minimum(nz[h], 0.0) - jnp.log(1.0 + jnp.exp2(jnp.abs(nz[h]) * (-LOG2E))) for h in hs]
        if diagonal:
            l1mb = [jnp.where(causal, l1mb[h], 0.0) for h in hs]
        cs = [_dot(_bf(l1mb[h]), tri) for h in hs]
        run = [run_ref[h] for h in hs]
        w = [jnp.exp(cs[h] + jnp.concatenate([run[h]] * (tile // LANES), axis=1) - nz[h]) for h in hs]
        if diagonal:
            w = [jnp.where(causal, w[h], 0.0) for h in hs]
        for h in hs:
            acc_ref[h] += _dot(_bf(w[h]), v_ref[pl.ds(ks, tile), cols[h]])
            run_ref[h] = run[h] + jnp.broadcast_to(cs[h][:, 0:1], run[h].shape)

    def alive():
        return (jnp.max(run_ref[...]) >= F32_EXP_UNDERFLOW_LOG).astype(jnp.int32)

    span(i, True)

    def body(carry):
        j, _ = carry
        span(i - 1 - j, False)
        return j + 1, alive()

    lax.while_loop(lambda c: (c[0] < i) & (c[1] > 0), body, (jnp.int32(0), alive()))
    for h in hs:
        o_ref[:, cols[h]] = _bf(acc_ref[h] * _silu(g_ref[:, cols[h]].astype(F32)))


def _stick_breaking(z, batch, seq):
    t = z.shape[0]
    width = SB_HEADS * SB_HEAD_DIM
    tile = min(SB_TILE, seq)
    nq = seq // tile
    base = 4
    return pl.pallas_call(
        functools.partial(_sb_kernel, tile=tile),
        out_shape=jax.ShapeDtypeStruct((t, width), BF16),
        grid=(batch, nq),
        in_specs=[
            pl.BlockSpec((tile, width), lambda b, i: (b * nq + i, base)),
            pl.BlockSpec((seq, width), lambda b, i: (b, base + 1)),
            pl.BlockSpec((seq, width), lambda b, i: (b, base + 2)),
            pl.BlockSpec((tile, width), lambda b, i: (b * nq + i, base + 3)),
        ],
        out_specs=pl.BlockSpec((tile, width), lambda b, i: (b * nq + i, 0)),
        scratch_shapes=[pltpu.VMEM((SB_HEADS, tile, SB_HEAD_DIM), F32), pltpu.VMEM((SB_HEADS, tile, LANES), F32)],
        compiler_params=_params("parallel", "arbitrary"),
        name="stick_breaking",
    )(z, z, z, z)


def _ple_tail(h1, p_ref, wg_ref, wp_ref, rest, final):
    gate = jax.nn.sigmoid(_dot(_bf(h1), wg_ref[...]))
    h2 = h1 + gate * _dot(_bf(p_ref[...]), wp_ref[...])
    if final:
        fg_ref, o_ref = rest
        o_ref[...] = _rms(h2, fg_ref[...])
    else:
        (o_ref,) = rest
        o_ref[...] = h2


def _even_out_kernel(h_ref, ma_ref, mb_ref, w_ref, p_ref, wg_ref, wp_ref, *rest, final):
    wa = ma_ref.shape[1]
    h1 = h_ref[...] + _dot(ma_ref[...], w_ref[:wa, :]) + _dot(mb_ref[...], w_ref[wa:, :])
    _ple_tail(h1, p_ref, wg_ref, wp_ref, rest, final)


def _odd_out_kernel(h_ref, y_ref, bonus_ref, gs_ref, lg_ref, lb_ref, w_ref, p_ref, wg_ref, wp_ref, *rest, final):
    ones_bd = _head_ones()
    inv_n = 1.0 / RWKV_HEAD_DIM
    y = y_ref[...]
    dev = y - _headsum(y, ones_bd) * inv_n
    var = _headsum(dev * dev, ones_bd) * inv_n
    yn = dev * lax.rsqrt(var + LNX_EPS) * lg_ref[...] + lb_ref[...]
    mixed = _bf((yn + bonus_ref[...]) * gs_ref[...].astype(F32))
    _ple_tail(h_ref[...] + _dot(mixed, w_ref[...]), p_ref, wg_ref, wp_ref, rest, final)


def _out_call(kern, name, h, row_args, full_args_mid, p, layer, w_gate, w_ple, final_gain):
    t, d = h.shape
    tm = min(ROW_TILE, t)
    final = final_gain is not None
    row = lambda a: pl.BlockSpec((tm, a.shape[1]), lambda i: (i, 0))
    full = lambda a: pl.BlockSpec(a.shape, lambda i: (0,) * a.ndim)
    args = [h, *row_args, *full_args_mid, p, w_gate, w_ple]
    p_spec = pl.BlockSpec((None, tm, p.shape[2]), lambda i: (layer, i, 0))
    specs = [row(h)] + [row(a) for a in row_args] + [full(a) for a in full_args_mid] + [p_spec, full(w_gate),
                                                                                        full(w_ple)]
    if final:
        args.append(final_gain.reshape(1, d))
        specs.append(full(args[-1]))
    return pl.pallas_call(
        functools.partial(kern, final=final),
        out_shape=jax.ShapeDtypeStruct((t, d), F32),
        grid=(t // tm,),
        in_specs=specs,
        out_specs=row(h),
        compiler_params=_params("parallel"),
        name=name,
    )(*args)


def _odd_in_kernel(*refs, vres, tiles_per_seq, chunk, sub):
    (h_ref, hp_ref, gain_ref, mu_ref, win_ref, w0_ref, w1_ref, w2_ref, a0_ref, a1_ref, a2_ref,
     kk_ref, ka_ref, rk_ref) = refs[:14]
    if vres:
        v0_ref, v1_ref, v2_ref, vf_ref = refs[14:18]
        outs = refs[18:]
    else:
        outs = refs[14:]
    rd_ref, kkd_ref, kinv_ref, binv_ref, kt_ref, bt_ref, vb_ref, wc_ref, bonus_ref, gs_ref = outs[:10]

    gain = gain_ref[...]
    hn_all = _rms(h_ref[...], gain)
    hnp = _rms(hp_ref[...], gain)
    tm = hn_all.shape[0]
    first = (pl.program_id(0) % tiles_per_seq) == 0
    prev_row = jnp.where(first, 0.0, hnp[SUBLANES - 1:SUBLANES, :])
    rowid = lax.broadcasted_iota(jnp.int32, hn_all.shape, 0)
    xx_all = jnp.where(rowid == 0, prev_row, pltpu.roll(hn_all, 1, 0)) - hn_all
    ones_bd = _head_ones()
    ka = ka_ref[...]
    ka_rest = 1.0 - ka
    ri = lax.broadcasted_iota(jnp.int32, (sub, sub), 0)
    ci = lax.broadcasted_iota(jnp.int32, (sub, sub), 1)
    sel = jnp.where(((ri // chunk) == (ci // chunk)) & (ri >= ci), 1.0, 0.0).astype(BF16)

    def part(r0):
        rows = slice(r0, r0 + sub)
        hn, xx = hn_all[rows], xx_all[rows]
        mix = lambda p: _bf(hn + xx * mu_ref[p:p + 1, :])
        r = _dot(mix(0), win_ref[0])
        yield
        k = _dot(mix(1), win_ref[1])
        yield
        xv = mix(2)
        v = _dot(xv, win_ref[2])
        yield
        g = _dot(mix(3), win_ref[3])
        yield
        xw = mix(4)
        xa = mix(5)
        w_in = w0_ref[...] + _dot(_bf(jnp.tanh(_dot(xw, w1_ref[...]))), w2_ref[...])
        lw = -DECAY_SCALE * jax.nn.sigmoid(w_in)
        a = jax.nn.sigmoid(a0_ref[...] + _dot(_bf(_dot(xa, a1_ref[...])), a2_ref[...]))
        if vres:
            mixv = jax.nn.sigmoid(v0_ref[...] + _dot(_bf(_dot(xv, v1_ref[...])), v2_ref[...]))
            v = v + (vf_ref[rows, :] - v) * mixv
        else:
            outs[10][rows, :] = v
        yield
        kkraw = k * kk_ref[...]
        kk = kkraw * lax.rsqrt(jnp.maximum(_headsum(kkraw * kkraw, ones_bd), 1e-24))
        kmod = k * (a * ka + ka_rest)
        b = kk * a
        hi, lo = _split(lw)
        cum = _dot(sel, hi) + _dot(sel, lo)
        ends = [jnp.exp(cum[(c + 1) * chunk - 1:(c + 1) * chunk, :]) for c in range(sub // chunk)]
        spread = lambda es: jnp.concatenate([jnp.broadcast_to(e, (chunk, e.shape[1])) for e in es], axis=0)
        wc, wc_neg = spread(ends), spread([-e for e in ends])
        yield
        w_inv = jnp.exp(-cum)
        kinv = kmod * w_inv
        binv = b * w_inv
        rd_ref[rows, :] = _bf(r * jnp.exp(cum))
        kkd_ref[rows, :] = _bf(kk * jnp.exp(cum - lw))
        kinv_ref[rows, :] = _bf(kinv)
        binv_ref[rows, :] = _bf(binv)
        yield
        kt_ref[rows, :] = _bf(kinv * wc)
        bt_ref[rows, :] = _bf(binv * wc_neg)
        vb_ref[rows, :] = _bf(v)
        for c in range(sub // chunk):
            wc_ref[r0 // chunk + c] = ends[c]
        bonus_ref[rows, :] = _headsum(r * kmod * rk_ref[...], ones_bd) * v
        gs_ref[rows, :] = _bf(_silu(g))
        yield

    parts = [part(r0) for r0 in range(0, tm, sub)]
    for _ in range(4):
        next(parts[0])
    for p in range(1, len(parts)):
        for _ in zip(parts[p - 1], parts[p]):
            pass
    for _ in parts[-1]:
        pass


def _odd_in_proj(h, seq, gain, mu, w_in, w0, w1, w2, a0, a1, a2, k_k, k_a, r_k, vres=None, v_first=None):
    t, d = h.shape
    tm = min(ODD_ROW_TILE, seq)
    chunk = SCAN_CHUNK
    tiles_per_seq = seq // tm
    sub_per_tile = tm // SUBLANES
    row = pl.BlockSpec((tm, d), lambda i: (i, 0))
    full = lambda a: pl.BlockSpec(a.shape, lambda i: (0,) * a.ndim)
    vec = lambda a: a.reshape(1, -1)
    args = [h, h, vec(gain), mu, w_in, vec(w0), w1, w2, vec(a0), a1, a2, vec(k_k), vec(k_a), vec(r_k)]
    specs = [row, pl.BlockSpec((SUBLANES, d), lambda i: (jnp.maximum(i * sub_per_tile - 1, 0), 0))]
    specs += [full(a) for a in args[2:]]
    if vres is not None:
        v0, v1, v2 = vres
        extra = [vec(v0), v1, v2]
        args += extra + [v_first]
        specs += [full(a) for a in extra] + [row]
    act = lambda dt: jax.ShapeDtypeStruct((t, d), dt)
    out_shape = [act(BF16)] * 7 + [jax.ShapeDtypeStruct((t // chunk, 1, d), F32), act(F32), act(BF16)]
    out_specs = [row] * 7 + [pl.BlockSpec((tm // chunk, 1, d), lambda i: (i, 0, 0)), row, row]
    if vres is None:
        out_shape.append(act(F32))
        out_specs.append(row)
    return pl.pallas_call(
        functools.partial(_odd_in_kernel, vres=vres is not None, tiles_per_seq=tiles_per_seq, chunk=chunk,
                          sub=min(ODD_SUB_ROWS, tm)),
        out_shape=out_shape,
        grid=(t // tm,),
        in_specs=specs,
        out_specs=out_specs,
        compiler_params=_params("parallel"),
        name="odd_in_proj",
    )(*args)


def _scan_kernel(rd_ref, kkd_ref, kinv_ref, binv_ref, kt_ref, bt_ref, v_ref, wc_ref, y_ref, state_ref, *, groups,
                 chunks):
    @pl.when(pl.program_id(0) == 0)
    def _():
        state_ref[...] = jnp.zeros_like(state_ref)

    batch = rd_ref.shape[0]
    c = rd_ref.shape[1] // chunks
    w = MXU_DIM
    hd = RWKV_HEAD_DIM
    row = lax.broadcasted_iota(jnp.int32, (c, w), 0)
    lane = lax.broadcasted_iota(jnp.int32, (c, w), 1)
    j = lane & (hd - 1)
    strict, incl, eye = j < row, j <= row, j == row
    blk16 = (j >> 4) == (row >> 4)
    blk32 = (j >> 5) == (row >> 5)
    head_of_lane = lane // hd
    head_masks = [head_of_lane == hh for hh in range(HEADS_PER_GROUP)]
    brow = lax.broadcasted_iota(jnp.int32, (w, w), 0)
    bcol = lax.broadcasted_iota(jnp.int32, (w, w), 1)
    bd = (brow // hd) == (bcol // hd)
    zero = jnp.zeros((), BF16)

    def stack(x):
        return jnp.concatenate([jnp.where(m, x, zero) for m in head_masks], axis=0)

    def pdot(x, y):
        return _dot(_bf(x), stack(_bf(y)))

    cols = [slice(g * w, (g + 1) * w) for g in range(groups)]
    rows = [slice(ch * c, (ch + 1) * c) for ch in range(chunks)]
    lanes = [(b, s) for b in range(batch) for s in cols]
    ix = [(b, r, s) for r in rows for b, s in lanes]
    ks = range(len(ix))
    rd = [rd_ref[i] for i in ix]
    kkd = [kkd_ref[i] for i in ix]
    v = [v_ref[i] for i in ix]
    lhs2 = [jnp.concatenate([kkd[k], rd[k]], axis=0) for k in ks]
    ak = [_dot_nt(lhs2[k], stack(kinv_ref[ix[k]])) for k in ks]
    ab = [_dot_nt(lhs2[k], stack(binv_ref[ix[k]])) for k in ks]
    a_kk = [_bf(jnp.where(strict, ak[k][:c], 0.0)) for k in ks]
    a_rk = [_bf(jnp.where(incl, ak[k][c:], 0.0)) for k in ks]
    n = [jnp.where(strict, ab[k][:c], 0.0) for k in ks]
    a_rb = [_bf(jnp.where(incl, ab[k][c:], 0.0)) for k in ks]

    nd = [jnp.where(blk16, n[k], 0.0) for k in ks]
    n2 = [pdot(nd[k], nd[k]) for k in ks]
    tinv = [jnp.where(eye, 1.0, 0.0) - nd[k] for k in ks]
    r1 = [pdot(jnp.concatenate([tinv[k], n2[k]], axis=0), n2[k]) for k in ks]
    tinv = [tinv[k] + r1[k][:c] for k in ks]
    r2 = [pdot(jnp.concatenate([tinv[k], r1[k][c:]], axis=0), r1[k][c:]) for k in ks]
    tinv = [tinv[k] + r2[k][:c] for k in ks]
    tinv = [tinv[k] + pdot(tinv[k], r2[k][c:]) for k in ks]
    off1 = [jnp.where(blk32 & jnp.logical_not(blk16), n[k], 0.0) for k in ks]
    m1 = [pdot(off1[k], tinv[k]) for k in ks]
    tinv = [tinv[k] - pdot(tinv[k], m1[k]) for k in ks]
    off2 = [jnp.where(blk32, 0.0, n[k]) for k in ks]
    m2 = [pdot(off2[k], tinv[k]) for k in ks]
    tinv = [_bf(tinv[k] - pdot(tinv[k], m2[k])) for k in ks]
    av = [_dot(jnp.concatenate([a_kk[k], a_rk[k]], axis=0), stack(v[k])) for k in ks]

    gs = range(len(lanes))
    state = [state_ref[g] for g in gs]
    for ch in range(chunks):
        chain = [ch * len(lanes) + g for g in gs]
        ps = [_dot_nt(lhs2[k], _bf(state[g])) for g, k in zip(gs, chain)]
        skb = [_bf(_dot(tinv[k], stack(_bf(ps[g][:c] + av[k][:c])))) for g, k in zip(gs, chain)]
        for g, k in zip(gs, chain):
            y_ref[ix[k]] = ps[g][c:] + av[k][c:] - _dot(a_rb[k], stack(skb[g]))
        upd = [_dot_tn(jnp.concatenate([v[k], skb[g]], axis=0),
                       jnp.concatenate([kt_ref[ix[k]], bt_ref[ix[k]]], axis=0))
               for g, k in zip(gs, chain)]
        state = [state[g] * wc_ref[lanes[g][0], ch, :, lanes[g][1]] + jnp.where(bd, upd[g], 0.0) for g in gs]
    for g in gs:
        state_ref[g] = state[g]


def _rwkv_scan(rd, kkd, kinv, binv, ktail, btail, vb, wc, batch, seq):
    t, d = rd.shape
    chunks = SCAN_CHUNKS_PER_STEP
    rows = SCAN_CHUNK * chunks
    groups = d // MXU_DIM
    blk = pl.BlockSpec((batch, rows, d), lambda n: (0, n, 0))
    seqs = [a.reshape(batch, seq, d) for a in (rd, kkd, kinv, binv, ktail, btail, vb)]
    y = pl.pallas_call(
        functools.partial(_scan_kernel, groups=groups, chunks=chunks),
        out_shape=jax.ShapeDtypeStruct((batch, seq, d), F32),
        grid=(seq // rows,),
        in_specs=[blk] * 7 + [pl.BlockSpec((batch, chunks, 1, d), lambda n: (0, n, 0, 0))],
        out_specs=blk,
        scratch_shapes=[pltpu.VMEM((batch * groups, MXU_DIM, MXU_DIM), F32)],
        compiler_params=_params("arbitrary"),
        name="rwkv7_scan",
    )(*seqs, wc.reshape(batch, seq // SCAN_CHUNK, 1, d))
    return y.reshape(t, d)


def kernel(x, p, norm_gain, final_gain, ple_proj, ple_gate, even_w_in, even_w_out, ret_gn_gain, odd_mu, odd_w_in, odd_w_out, rwkv_w0, rwkv_w1, rwkv_w2, rwkv_a0, rwkv_a1, rwkv_a2, rwkv_v0, rwkv_v1, rwkv_v2, rwkv_k_k, rwkv_k_a, rwkv_r_k, rwkv_lnx_gain, rwkv_lnx_bias):
    batch, seq, d = x.shape
    depth = p.shape[0]
    t = batch * seq
    h = x.reshape(t, d)
    p2 = p.reshape(depth, t, p.shape[-1])
    vec = lambda a: a.reshape(1, -1)
    v_first = None
    for i in range(depth):
        final = final_gain if i == depth - 1 else None
        w_gate, w_ple = _bf(ple_gate[i]), _bf(ple_proj[i])
        if i % 2 == 0:
            e = i // 2
            z = _even_in_proj(h, norm_gain[i], _bf(even_w_in[e]))
            mixed_a = _retention(z, batch, seq, ret_gn_gain[e])
            mixed_b = _stick_breaking(z, batch, seq)
            h = _out_call(_even_out_kernel, "even_out_proj_ple", h, [mixed_a, mixed_b], [_bf(even_w_out[e])],
                          p2, i, w_gate, w_ple, final)
        else:
            o = i // 2
            vres = None if v_first is None else (rwkv_v0[o - 1], _bf(rwkv_v1[o - 1]), _bf(rwkv_v2[o - 1]))
            outs = _odd_in_proj(
                h, seq, norm_gain[i], odd_mu[o], _bf(odd_w_in[o]), rwkv_w0[o], _bf(rwkv_w1[o]), _bf(rwkv_w2[o]),
                rwkv_a0[o], _bf(rwkv_a1[o]), _bf(rwkv_a2[o]), rwkv_k_k[o], rwkv_k_a[o], rwkv_r_k[o], vres, v_first)
            rd, kkd, kinv, binv, ktail, btail, vb, wc, bonus, gs = outs[:10]
            if v_first is None:
                v_first = outs[10]
            y = _rwkv_scan(rd, kkd, kinv, binv, ktail, btail, vb, wc, batch, seq)
            h = _out_call(_odd_out_kernel, "odd_out_proj_ple", h, [y, bonus, gs],
                          [vec(rwkv_lnx_gain[o]), vec(rwkv_lnx_bias[o]), _bf(odd_w_out[o])], p2, i, w_gate, w_ple,
                          final)
    return h.reshape(batch, seq, d)
```

```python
import functools
import math

import jax
import jax.numpy as jnp
import numpy as np
from jax import lax
from jax.experimental import pallas as pl
from jax.experimental.pallas import tpu as pltpu

F32 = jnp.float32
BF16 = jnp.bfloat16

LANES = 128
SUBLANES = 8
MXU_DIM = 256
VMEM_LIMIT_BYTES = 56 * 1024 * 1024

RET_HEADS = 4
RET_HEAD_DIM = 128
SB_HEADS = 4
SB_HEAD_DIM = 128
ROPE_BASE = 10000.0
RWKV_HEAD_DIM = 64
RMS_EPS = 1e-6
GN_EPS = 1e-5
LNX_EPS = 64e-5
F32_EXP_UNDERFLOW_LOG = -110.0
DECAY_SCALE = math.exp(-0.5)
LOG2E = math.log2(math.e)

ROW_TILE = 1024
OUT_ROW_TILE = 1024
ODD_ROW_TILE = 512
ODD_SUB_ROWS = 256
IN_COL_TILE = 1024
RET_CHUNK = 256
SB_TILE = 256
SCAN_CHUNK = 64
SCAN_CHUNKS_PER_STEP = 4
HEADS_PER_GROUP = MXU_DIM // RWKV_HEAD_DIM


def _bf(x):
    return x.astype(BF16)


def _dot(a, b):
    return jnp.dot(a, b, preferred_element_type=F32)


def _dot_nt(a, b):
    return lax.dot_general(a, b, (((1,), (1,)), ((), ())), preferred_element_type=F32)


def _dot_tn(a, b):
    return lax.dot_general(a, b, (((0,), (0,)), ((), ())), preferred_element_type=F32)


def _split(x):
    hi = _bf(x)
    lo = _bf(x - hi.astype(F32))
    return hi, lo


def _rms(x, gain):
    return x * lax.rsqrt(jnp.mean(x * x, axis=-1, keepdims=True) + RMS_EPS) * gain


def _silu(g):
    return g * jax.nn.sigmoid(g)


def _head_ones():
    r = lax.broadcasted_iota(jnp.int32, (MXU_DIM, MXU_DIM), 0)
    c = lax.broadcasted_iota(jnp.int32, (MXU_DIM, MXU_DIM), 1)
    return jnp.where((r // RWKV_HEAD_DIM) == (c // RWKV_HEAD_DIM), 1.0, 0.0).astype(BF16)


def _headsum(x, ones_bd):
    xb = _bf(x)
    cols = [_dot(xb[:, s * MXU_DIM:(s + 1) * MXU_DIM], ones_bd) for s in range(x.shape[1] // MXU_DIM)]
    return jnp.concatenate(cols, axis=1)


def _params(*sem):
    return pltpu.CompilerParams(dimension_semantics=sem, vmem_limit_bytes=VMEM_LIMIT_BYTES)


def _even_in_kernel(h_ref, g_ref, w_ref, z_ref):
    hn = _bf(_rms(h_ref[...], g_ref[...]))
    for j in range(w_ref.shape[1] // IN_COL_TILE):
        sl = slice(j * IN_COL_TILE, (j + 1) * IN_COL_TILE)
        z_ref[:, sl] = _bf(_dot(hn, w_ref[:, sl]))


def _even_in_proj(h, gain, w_in):
    t, d = h.shape
    f = w_in.shape[1]
    tm = min(ROW_TILE, t)
    return pl.pallas_call(
        _even_in_kernel,
        out_shape=jax.ShapeDtypeStruct((t, f), BF16),
        grid=(t // tm,),
        in_specs=[
            pl.BlockSpec((tm, d), lambda i: (i, 0)),
            pl.BlockSpec((1, d), lambda i: (0, 0)),
            pl.BlockSpec((d, f), lambda i: (0, 0)),
        ],
        out_specs=pl.BlockSpec((tm, f), lambda i: (i, 0)),
        compiler_params=_params("parallel"),
        name="even_in_proj",
    )(h, gain.reshape(1, d), w_in)


def _ret_kernel(q_ref, k_ref, v_ref, g_ref, cos_ref, sin_ref, intra_ref, qdec_ref, kdec_ref, cdec_ref,
                gn_ref, o_ref, state_ref):
    @pl.when(pl.program_id(1) == 0)
    def _():
        state_ref[...] = jnp.zeros_like(state_ref)

    hd = RET_HEAD_DIM
    half = hd // 2
    cos = cos_ref[...]
    sin = sin_ref[...]
    hs = range(RET_HEADS)
    sl = [slice(h * hd, (h + 1) * hd) for h in hs]
    q = [q_ref[:, s].astype(F32) for s in sl]
    k = [k_ref[:, s].astype(F32) for s in sl]
    qr = [q[h] * cos + pltpu.roll(q[h], half, 1) * sin for h in hs]
    kr = [(k[h] * cos + pltpu.roll(k[h], half, 1) * sin) * (hd ** -0.5) for h in hs]
    v = [v_ref[:, s] for s in sl]
    scores = [_dot_nt(_bf(qr[h]), _bf(kr[h])) * intra_ref[h] for h in hs]
    state = [state_ref[h] for h in hs]
    out = [_dot(_bf(scores[h]), v[h]) + _dot(_bf(qr[h] * qdec_ref[h]), _bf(state[h])) for h in hs]
    for h in hs:
        state_ref[h] = state[h] * cdec_ref[h, 0:1, :] + _dot_tn(_bf(kr[h] * kdec_ref[h]), v[h])
    for h in hs:
        mu = jnp.mean(out[h], axis=-1, keepdims=True)
        dev = out[h] - mu
        var = jnp.mean(dev * dev, axis=-1, keepdims=True)
        y = dev * lax.rsqrt(var + GN_EPS) * gn_ref[:, sl[h]]
        o_ref[:, sl[h]] = _bf(y * _silu(g_ref[:, sl[h]].astype(F32)))


def _retention(z, batch, seq, gn_gain):
    t = z.shape[0]
    c = min(RET_CHUNK, seq)
    nc = seq // c
    hd = RET_HEAD_DIM
    half = hd // 2
    f32 = np.float32
    inv_freq = f32(ROPE_BASE) ** (-np.arange(half, dtype=f32) / f32(half))
    ang = np.arange(seq, dtype=f32)[:, None] * inv_freq[None, :]
    cos2 = np.concatenate([np.cos(ang), np.cos(ang)], axis=-1)
    sin2 = np.concatenate([-np.sin(ang), np.sin(ang)], axis=-1)
    lg = np.log(f32(1.0) - f32(2.0) ** (f32(-5.0) - np.arange(RET_HEADS, dtype=f32)))
    ci = np.arange(c, dtype=f32)
    diff = ci[:, None] - ci[None, :]
    intra = np.where(diff[None] >= 0, np.exp(np.maximum(diff, f32(0.0))[None] * lg[:, None, None]), f32(0.0))
    qdec = np.ascontiguousarray(np.broadcast_to(np.exp((ci + 1)[None, :, None] * lg[:, None, None]),
                                                (RET_HEADS, c, hd)))
    kdec = np.ascontiguousarray(np.broadcast_to(np.exp((c - 1 - ci)[None, :, None] * lg[:, None, None]),
                                                (RET_HEADS, c, hd)))
    cdec = np.ascontiguousarray(np.broadcast_to(np.exp(f32(c) * lg)[:, None, None], (RET_HEADS, SUBLANES, hd)))
    cos2, sin2, intra, qdec, kdec, cdec = (jnp.asarray(a, F32) for a in (cos2, sin2, intra, qdec, kdec, cdec))

    width = RET_HEADS * hd

    def zcol(j):
        return pl.BlockSpec((c, width), lambda b, n: (b * nc + n, j))

    full = lambda a: pl.BlockSpec(a.shape, lambda b, n: (0,) * a.ndim)
    gn = gn_gain.reshape(1, -1)
    return pl.pallas_call(
        _ret_kernel,
        out_shape=jax.ShapeDtypeStruct((t, width), BF16),
        grid=(batch, nc),
        in_specs=[
            zcol(0), zcol(1), zcol(2), zcol(3),
            pl.BlockSpec((c, hd), lambda b, n: (n, 0)),
            pl.BlockSpec((c, hd), lambda b, n: (n, 0)),
            full(intra), full(qdec), full(kdec), full(cdec), full(gn),
        ],
        out_specs=pl.BlockSpec((c, width), lambda b, n: (b * nc + n, 0)),
        scratch_shapes=[pltpu.VMEM((RET_HEADS, hd, hd), F32)],
        compiler_params=_params("parallel", "arbitrary"),
        name="retention",
    )(z, z, z, z, cos2, sin2, intra, qdec, kdec, cdec, gn)


def _sb_kernel(q_ref, k_ref, v_ref, g_ref, o_ref, acc_ref, run_ref, *, tile):
    i = pl.program_id(1)
    hd = SB_HEAD_DIM
    scale = hd ** -0.5
    hs = range(SB_HEADS)
    cols = [slice(h * hd, (h + 1) * hd) for h in hs]
    q = [q_ref[:, s] for s in cols]
    trow = lax.broadcasted_iota(jnp.int32, (tile, tile), 0)
    tcol = lax.broadcasted_iota(jnp.int32, (tile, tile), 1)
    tri = jnp.where(trow >= tcol, 1.0, 0.0).astype(BF16)
    causal = tcol < trow

    acc_ref[...] = jnp.zeros_like(acc_ref)
    run_ref[...] = jnp.zeros_like(run_ref)

    def span(kb, diagonal):
        ks = pl.multiple_of(kb * tile, tile)
        nz = [_dot_nt(q[h], k_ref[pl.ds(ks, tile), cols[h]]) * (-scale) for h in hs]
        l1mb = [jnp.minimum(nz[h], 0.0) - jnp.log(1.0 + jnp.exp2(jnp.abs(nz[h]) * (-LOG2E))) for h in hs]
        if diagonal:
            l1mb = [jnp.where(causal, l1mb[h], 0.0) for h in hs]
        cs = [_dot(_bf(l1mb[h]), tri) for h in hs]
        run = [run_ref[h] for h in hs]
        w = [jnp.exp(cs[h] + jnp.concatenate([run[h]] * (tile // LANES), axis=1) - nz[h]) for h in hs]
        if diagonal:
            w = [jnp.where(causal, w[h], 0.0) for h in hs]
        for h in hs:
            acc_ref[h] += _dot(_bf(w[h]), v_ref[pl.ds(ks, tile), cols[h]])
            run_ref[h] = run[h] + jnp.broadcast_to(cs[h][:, 0:1], run[h].shape)

    def alive():
        return (jnp.max(run_ref[...]) >= F32_EXP_UNDERFLOW_LOG).astype(jnp.int32)

    span(i, True)

    def body(carry):
        j, _ = carry
        span(i - 1 - j, False)
        return j + 1, alive()

    lax.while_loop(lambda c: (c[0] < i) & (c[1] > 0), body, (jnp.int32(0), alive()))
    for h in hs:
        o_ref[:, cols[h]] = _bf(acc_ref[h] * _silu(g_ref[:, cols[h]].astype(F32)))


def _stick_breaking(z, batch, seq):
    t = z.shape[0]
    width = SB_HEADS * SB_HEAD_DIM
    tile = min(SB_TILE, seq)
    nq = seq // tile
    base = 4
    return pl.pallas_call(
        functools.partial(_sb_kernel, tile=tile),
        out_shape=jax.ShapeDtypeStruct((t, width), BF16),
        grid=(batch, nq),
        in_specs=[
            pl.BlockSpec((tile, width), lambda b, i: (b * nq + i, base)),
            pl.BlockSpec((seq, width), lambda b, i: (b, base + 1)),
            pl.BlockSpec((seq, width), lambda b, i: (b, base + 2)),
            pl.BlockSpec((tile, width), lambda b, i: (b * nq + i, base + 3)),
        ],
        out_specs=pl.BlockSpec((tile, width), lambda b, i: (b * nq + i, 0)),
        scratch_shapes=[pltpu.VMEM((SB_HEADS, tile, SB_HEAD_DIM), F32), pltpu.VMEM((SB_HEADS, tile, LANES), F32)],
        compiler_params=_params("parallel", "arbitrary"),
        name="stick_breaking",
    )(z, z, z, z)


def _ple_tail(h1, p_ref, wg_ref, wp_ref, rest, final):
    gate = jax.nn.sigmoid(_dot(_bf(h1), wg_ref[...]))
    h2 = h1 + gate * _dot(_bf(p_ref[...]), wp_ref[...])
    if final:
        fg_ref, o_ref = rest
        o_ref[...] = _rms(h2, fg_ref[...])
    else:
        (o_ref,) = rest
        o_ref[...] = h2


def _even_out_kernel(h_ref, ma_ref, mb_ref, w_ref, p_ref, wg_ref, wp_ref, *rest, final):
    wa = ma_ref.shape[1]
    h1 = h_ref[...] + _dot(ma_ref[...], w_ref[:wa, :]) + _dot(mb_ref[...], w_ref[wa:, :])
    _ple_tail(h1, p_ref, wg_ref, wp_ref, rest, final)


def _odd_out_kernel(h_ref, y_ref, bonus_ref, gs_ref, lg_ref, lb_ref, w_ref, p_ref, wg_ref, wp_ref, *rest, final):
    ones_bd = _head_ones()
    inv_n = 1.0 / RWKV_HEAD_DIM
    y = y_ref[...]
    dev = y - _headsum(y, ones_bd) * inv_n
    var = _headsum(dev * dev, ones_bd) * inv_n
    yn = dev * lax.rsqrt(var + LNX_EPS) * lg_ref[...] + lb_ref[...]
    mixed = _bf((yn + bonus_ref[...]) * gs_ref[...].astype(F32))
    _ple_tail(h_ref[...] + _dot(mixed, w_ref[...]), p_ref, wg_ref, wp_ref, rest, final)


def _out_call(kern, name, h, row_args, full_args_mid, p, layer, w_gate, w_ple, final_gain):
    t, d = h.shape
    tm = min(OUT_ROW_TILE, t)
    final = final_gain is not None
    row = lambda a: pl.BlockSpec((tm, a.shape[1]), lambda i: (i, 0))
    full = lambda a: pl.BlockSpec(a.shape, lambda i: (0,) * a.ndim)
    args = [h, *row_args, *full_args_mid, p, w_gate, w_ple]
    p_spec = pl.BlockSpec((None, tm, p.shape[2]), lambda i: (layer, i, 0))
    specs = [row(h)] + [row(a) for a in row_args] + [full(a) for a in full_args_mid] + [p_spec, full(w_gate),
                                                                                        full(w_ple)]
    if final:
        args.append(final_gain.reshape(1, d))
        specs.append(full(args[-1]))
    return pl.pallas_call(
        functools.partial(kern, final=final),
        out_shape=jax.ShapeDtypeStruct((t, d), F32),
        grid=(t // tm,),
        in_specs=specs,
        out_specs=row(h),
        compiler_params=_params("parallel"),
        name=name,
    )(*args)


def _odd_in_kernel(*refs, vres, tiles_per_seq, chunk, sub):
    (h_ref, hp_ref, gain_ref, mu_ref, win_ref, w0_ref, w1_ref, w2_ref, a0_ref, a1_ref, a2_ref,
     kk_ref, ka_ref, rk_ref) = refs[:14]
    if vres:
        v0_ref, v1_ref, v2_ref, vf_ref = refs[14:18]
        outs = refs[18:]
    else:
        outs = refs[14:]
    rd_ref, kkd_ref, kinv_ref, binv_ref, kt_ref, bt_ref, vb_ref, wc_ref, bonus_ref, gs_ref = outs[:10]

    gain = gain_ref[...]
    hn_all = _rms(h_ref[...], gain)
    hnp = _rms(hp_ref[...], gain)
    tm = hn_all.shape[0]
    first = (pl.program_id(0) % tiles_per_seq) == 0
    prev_row = jnp.where(first, 0.0, hnp[SUBLANES - 1:SUBLANES, :])
    rowid = lax.broadcasted_iota(jnp.int32, hn_all.shape, 0)
    xx_all = jnp.where(rowid == 0, prev_row, pltpu.roll(hn_all, 1, 0)) - hn_all
    ones_bd = _head_ones()
    ka = ka_ref[...]
    ka_rest = 1.0 - ka
    ri = lax.broadcasted_iota(jnp.int32, (sub, sub), 0)
    ci = lax.broadcasted_iota(jnp.int32, (sub, sub), 1)
    sel = jnp.where(((ri // chunk) == (ci // chunk)) & (ri >= ci), 1.0, 0.0).astype(BF16)

    def part(r0):
        rows = slice(r0, r0 + sub)
        hn, xx = hn_all[rows], xx_all[rows]
        mix = lambda p: _bf(hn + xx * mu_ref[p:p + 1, :])
        r = _dot(mix(0), win_ref[0])
        yield
        k = _dot(mix(1), win_ref[1])
        yield
        xv = mix(2)
        v = _dot(xv, win_ref[2])
        yield
        g = _dot(mix(3), win_ref[3])
        yield
        xw = mix(4)
        xa = mix(5)
        w_in = w0_ref[...] + _dot(_bf(jnp.tanh(_dot(xw, w1_ref[...]))), w2_ref[...])
        lw = -DECAY_SCALE * jax.nn.sigmoid(w_in)
        a = jax.nn.sigmoid(a0_ref[...] + _dot(_bf(_dot(xa, a1_ref[...])), a2_ref[...]))
        if vres:
            mixv = jax.nn.sigmoid(v0_ref[...] + _dot(_bf(_dot(xv, v1_ref[...])), v2_ref[...]))
            v = v + (vf_ref[rows, :] - v) * mixv
        else:
            outs[10][rows, :] = v
        yield
        kkraw = k * kk_ref[...]
        kk = kkraw * lax.rsqrt(jnp.maximum(_headsum(kkraw * kkraw, ones_bd), 1e-24))
        kmod = k * (a * ka + ka_rest)
        b = kk * a
        hi, lo = _split(lw)
        cum = _dot(sel, hi) + _dot(sel, lo)
        ends = [jnp.exp(cum[(c + 1) * chunk - 1:(c + 1) * chunk, :]) for c in range(sub // chunk)]
        spread = lambda es: jnp.concatenate([jnp.broadcast_to(e, (chunk, e.shape[1])) for e in es], axis=0)
        wc, wc_neg = spread(ends), spread([-e for e in ends])
        yield
        w_inv = jnp.exp(-cum)
        kinv = kmod * w_inv
        binv = b * w_inv
        rd_ref[rows, :] = _bf(r * jnp.exp(cum))
        kkd_ref[rows, :] = _bf(kk * jnp.exp(cum - lw))
        kinv_ref[rows, :] = _bf(kinv)
        binv_ref[rows, :] = _bf(binv)
        yield
        kt_ref[rows, :] = _bf(kinv * wc)
        bt_ref[rows, :] = _bf(binv * wc_neg)
        vb_ref[rows, :] = _bf(v)
        for c in range(sub // chunk):
            wc_ref[r0 // chunk + c] = ends[c]
        bonus_ref[rows, :] = _headsum(r * kmod * rk_ref[...], ones_bd) * v
        gs_ref[rows, :] = _bf(_silu(g))
        yield

    parts = [part(r0) for r0 in range(0, tm, sub)]
    for _ in range(4):
        next(parts[0])
    for p in range(1, len(parts)):
        for _ in zip(parts[p - 1], parts[p]):
            pass
    for _ in parts[-1]:
        pass


def _odd_in_proj(h, seq, gain, mu, w_in, w0, w1, w2, a0, a1, a2, k_k, k_a, r_k, vres=None, v_first=None):
    t, d = h.shape
    tm = min(ODD_ROW_TILE, seq)
    chunk = SCAN_CHUNK
    tiles_per_seq = seq // tm
    sub_per_tile = tm // SUBLANES
    row = pl.BlockSpec((tm, d), lambda i: (i, 0))
    full = lambda a: pl.BlockSpec(a.shape, lambda i: (0,) * a.ndim)
    vec = lambda a: a.reshape(1, -1)
    args = [h, h, vec(gain), mu, w_in, vec(w0), w1, w2, vec(a0), a1, a2, vec(k_k), vec(k_a), vec(r_k)]
    specs = [row, pl.BlockSpec((SUBLANES, d), lambda i: (jnp.maximum(i * sub_per_tile - 1, 0), 0))]
    specs += [full(a) for a in args[2:]]
    if vres is not None:
        v0, v1, v2 = vres
        extra = [vec(v0), v1, v2]
        args += extra + [v_first]
        specs += [full(a) for a in extra] + [row]
    act = lambda dt: jax.ShapeDtypeStruct((t, d), dt)
    out_shape = [act(BF16)] * 7 + [jax.ShapeDtypeStruct((t // chunk, 1, d), F32), act(F32), act(BF16)]
    out_specs = [row] * 7 + [pl.BlockSpec((tm // chunk, 1, d), lambda i: (i, 0, 0)), row, row]
    if vres is None:
        out_shape.append(act(F32))
        out_specs.append(row)
    return pl.pallas_call(
        functools.partial(_odd_in_kernel, vres=vres is not None, tiles_per_seq=tiles_per_seq, chunk=chunk,
                          sub=min(ODD_SUB_ROWS, tm)),
        out_shape=out_shape,
        grid=(t // tm,),
        in_specs=specs,
        out_specs=out_specs,
        compiler_params=_params("parallel"),
        name="odd_in_proj",
    )(*args)


def _scan_kernel(rd_ref, kkd_ref, kinv_ref, binv_ref, kt_ref, bt_ref, v_ref, wc_ref, y_ref, state_ref, *, groups,
                 chunks):
    @pl.when(pl.program_id(0) == 0)
    def _():
        state_ref[...] = jnp.zeros_like(state_ref)

    batch = rd_ref.shape[0]
    c = rd_ref.shape[1] // chunks
    w = MXU_DIM
    hd = RWKV_HEAD_DIM
    row = lax.broadcasted_iota(jnp.int32, (c, w), 0)
    lane = lax.broadcasted_iota(jnp.int32, (c, w), 1)
    j = lane & (hd - 1)
    strict, incl, eye = j < row, j <= row, j == row
    blk16 = (j >> 4) == (row >> 4)
    blk32 = (j >> 5) == (row >> 5)
    head_of_lane = lane // hd
    head_masks = [head_of_lane == hh for hh in range(HEADS_PER_GROUP)]
    brow = lax.broadcasted_iota(jnp.int32, (w, w), 0)
    bcol = lax.broadcasted_iota(jnp.int32, (w, w), 1)
    bd = (brow // hd) == (bcol // hd)
    zero = jnp.zeros((), BF16)

    def stack(x):
        return jnp.concatenate([jnp.where(m, x, zero) for m in head_masks], axis=0)

    def pdot(x, y):
        return _dot(_bf(x), stack(_bf(y)))

    cols = [slice(g * w, (g + 1) * w) for g in range(groups)]
    rows = [slice(ch * c, (ch + 1) * c) for ch in range(chunks)]
    lanes = [(b, s) for b in range(batch) for s in cols]
    ix = [(b, r, s) for r in rows for b, s in lanes]
    ks = range(len(ix))
    rd = [rd_ref[i] for i in ix]
    kkd = [kkd_ref[i] for i in ix]
    v = [v_ref[i] for i in ix]
    lhs2 = [jnp.concatenate([kkd[k], rd[k]], axis=0) for k in ks]
    ak = [_dot_nt(lhs2[k], stack(kinv_ref[ix[k]])) for k in ks]
    ab = [_dot_nt(lhs2[k], stack(binv_ref[ix[k]])) for k in ks]
    a_kk = [_bf(jnp.where(strict, ak[k][:c], 0.0)) for k in ks]
    a_rk = [_bf(jnp.where(incl, ak[k][c:], 0.0)) for k in ks]
    n = [jnp.where(strict, ab[k][:c], 0.0) for k in ks]
    a_rb = [_bf(jnp.where(incl, ab[k][c:], 0.0)) for k in ks]

    nd = [jnp.where(blk16, n[k], 0.0) for k in ks]
    n2 = [pdot(nd[k], nd[k]) for k in ks]
    tinv = [jnp.where(eye, 1.0, 0.0) - nd[k] for k in ks]
    r1 = [pdot(jnp.concatenate([tinv[k], n2[k]], axis=0), n2[k]) for k in ks]
    tinv = [tinv[k] + r1[k][:c] for k in ks]
    r2 = [pdot(jnp.concatenate([tinv[k], r1[k][c:]], axis=0), r1[k][c:]) for k in ks]
    tinv = [tinv[k] + r2[k][:c] for k in ks]
    tinv = [tinv[k] + pdot(tinv[k], r2[k][c:]) for k in ks]
    off1 = [jnp.where(blk32 & jnp.logical_not(blk16), n[k], 0.0) for k in ks]
    m1 = [pdot(off1[k], tinv[k]) for k in ks]
    tinv = [tinv[k] - pdot(tinv[k], m1[k]) for k in ks]
    off2 = [jnp.where(blk32, 0.0, n[k]) for k in ks]
    m2 = [pdot(off2[k], tinv[k]) for k in ks]
    tinv = [_bf(tinv[k] - pdot(tinv[k], m2[k])) for k in ks]
    av = [_dot(jnp.concatenate([a_kk[k], a_rk[k]], axis=0), stack(v[k])) for k in ks]

    gs = range(len(lanes))
    state = [state_ref[g] for g in gs]
    for ch in range(chunks):
        chain = [ch * len(lanes) + g for g in gs]
        ps = [_dot_nt(lhs2[k], _bf(state[g])) for g, k in zip(gs, chain)]
        skb = [_bf(_dot(tinv[k], stack(_bf(ps[g][:c] + av[k][:c])))) for g, k in zip(gs, chain)]
        for g, k in zip(gs, chain):
            y_ref[ix[k]] = ps[g][c:] + av[k][c:] - _dot(a_rb[k], stack(skb[g]))
        upd = [_dot_tn(jnp.concatenate([v[k], skb[g]], axis=0),
                       jnp.concatenate([kt_ref[ix[k]], bt_ref[ix[k]]], axis=0))
               for g, k in zip(gs, chain)]
        state = [state[g] * wc_ref[lanes[g][0], ch, :, lanes[g][1]] + jnp.where(bd, upd[g], 0.0) for g in gs]
    for g in gs:
        state_ref[g] = state[g]


def _rwkv_scan(rd, kkd, kinv, binv, ktail, btail, vb, wc, batch, seq):
    t, d = rd.shape
    chunks = SCAN_CHUNKS_PER_STEP
    rows = SCAN_CHUNK * chunks
    groups = d // MXU_DIM
    blk = pl.BlockSpec((batch, rows, d), lambda n: (0, n, 0))
    seqs = [a.reshape(batch, seq, d) for a in (rd, kkd, kinv, binv, ktail, btail, vb)]
    y = pl.pallas_call(
        functools.partial(_scan_kernel, groups=groups, chunks=chunks),
        out_shape=jax.ShapeDtypeStruct((batch, seq, d), F32),
        grid=(seq // rows,),
        in_specs=[blk] * 7 + [pl.BlockSpec((batch, chunks, 1, d), lambda n: (0, n, 0, 0))],
        out_specs=blk,
        scratch_shapes=[pltpu.VMEM((batch * groups, MXU_DIM, MXU_DIM), F32)],
        compiler_params=_params("arbitrary"),
        name="rwkv7_scan",
    )(*seqs, wc.reshape(batch, seq // SCAN_CHUNK, 1, d))
    return y.reshape(t, d)


def kernel(x, p, norm_gain, final_gain, ple_proj, ple_gate, even_w_in, even_w_out, ret_gn_gain, odd_mu, odd_w_in, odd_w_out, rwkv_w0, rwkv_w1, rwkv_w2, rwkv_a0, rwkv_a1, rwkv_a2, rwkv_v0, rwkv_v1, rwkv_v2, rwkv_k_k, rwkv_k_a, rwkv_r_k, rwkv_lnx_gain, rwkv_lnx_bias):
    batch, seq, d = x.shape
    depth = p.shape[0]
    t = batch * seq
    h = x.reshape(t, d)
    p2 = p.reshape(depth, t, p.shape[-1])
    vec = lambda a: a.reshape(1, -1)
    v_first = None
    for i in range(depth):
        final = final_gain if i == depth - 1 else None
        w_gate, w_ple = _bf(ple_gate[i]), _bf(ple_proj[i])
        if i % 2 == 0:
            e = i // 2
            z = _even_in_proj(h, norm_gain[i], _bf(even_w_in[e]))
            mixed_a = _retention(z, batch, seq, ret_gn_gain[e])
            mixed_b = _stick_breaking(z, batch, seq)
            h = _out_call(_even_out_kernel, "even_out_proj_ple", h, [mixed_a, mixed_b], [_bf(even_w_out[e])],
                          p2, i, w_gate, w_ple, final)
        else:
            o = i // 2
            vres = None if v_first is None else (rwkv_v0[o - 1], _bf(rwkv_v1[o - 1]), _bf(rwkv_v2[o - 1]))
            outs = _odd_in_proj(
                h, seq, norm_gain[i], odd_mu[o], _bf(odd_w_in[o]), rwkv_w0[o], _bf(rwkv_w1[o]), _bf(rwkv_w2[o]),
                rwkv_a0[o], _bf(rwkv_a1[o]), _bf(rwkv_a2[o]), rwkv_k_k[o], rwkv_k_a[o], rwkv_r_k[o], vres, v_first)
            rd, kkd, kinv, binv, ktail, btail, vb, wc, bonus, gs = outs[:10]
            if v_first is None:
                v_first = outs[10]
            y = _rwkv_scan(rd, kkd, kinv, binv, ktail, btail, vb, wc, batch, seq)
            h = _out_call(_odd_out_kernel, "odd_out_proj_ple", h, [y, bonus, gs],
                          [vec(rwkv_lnx_gain[o]), vec(rwkv_lnx_bias[o]), _bf(odd_w_out[o])], p2, i, w_gate, w_ple,
                          final)
    return h.reshape(batch, seq, d)
```

```python
import functools
import math

import jax
import jax.numpy as jnp
import numpy as np
from jax import lax
from jax.experimental import pallas as pl
from jax.experimental.pallas import tpu as pltpu

F32 = jnp.float32
BF16 = jnp.bfloat16

LANES = 128
SUBLANES = 8
MXU_DIM = 256
VMEM_LIMIT_BYTES = 56 * 1024 * 1024

RET_HEADS = 4
RET_HEAD_DIM = 128
SB_HEADS = 4
SB_HEAD_DIM = 128
ROPE_BASE = 10000.0
RWKV_HEAD_DIM = 64
RMS_EPS = 1e-6
GN_EPS = 1e-5
LNX_EPS = 64e-5
F32_EXP_UNDERFLOW_LOG = -110.0
DECAY_SCALE = math.exp(-0.5)
LOG2E = math.log2(math.e)

ROW_TILE = 1024
OUT_ROW_TILE = 1024
ODD_ROW_TILE = 512
ODD_SUB_ROWS = 256
IN_COL_TILE = 1024
RET_CHUNK = 256
SB_TILE = 256
SCAN_CHUNK = 64
SCAN_CHUNKS_PER_STEP = 2
HEADS_PER_GROUP = MXU_DIM // RWKV_HEAD_DIM


def _bf(x):
    return x.astype(BF16)


def _dot(a, b):
    return jnp.dot(a, b, preferred_element_type=F32)


def _dot_nt(a, b):
    return lax.dot_general(a, b, (((1,), (1,)), ((), ())), preferred_element_type=F32)


def _dot_tn(a, b):
    return lax.dot_general(a, b, (((0,), (0,)), ((), ())), preferred_element_type=F32)


def _split(x):
    hi = _bf(x)
    lo = _bf(x - hi.astype(F32))
    return hi, lo


def _rms(x, gain):
    return x * lax.rsqrt(jnp.mean(x * x, axis=-1, keepdims=True) + RMS_EPS) * gain


def _silu(g):
    return g * jax.nn.sigmoid(g)


def _head_ones():
    r = lax.broadcasted_iota(jnp.int32, (MXU_DIM, MXU_DIM), 0)
    c = lax.broadcasted_iota(jnp.int32, (MXU_DIM, MXU_DIM), 1)
    return jnp.where((r // RWKV_HEAD_DIM) == (c // RWKV_HEAD_DIM), 1.0, 0.0).astype(BF16)


def _headsum(x, ones_bd):
    xb = _bf(x)
    cols = [_dot(xb[:, s * MXU_DIM:(s + 1) * MXU_DIM], ones_bd) for s in range(x.shape[1] // MXU_DIM)]
    return jnp.concatenate(cols, axis=1)


def _params(*sem):
    return pltpu.CompilerParams(dimension_semantics=sem, vmem_limit_bytes=VMEM_LIMIT_BYTES)


def _even_in_kernel(h_ref, g_ref, w_ref, z_ref):
    hn = _bf(_rms(h_ref[...], g_ref[...]))
    for j in range(w_ref.shape[1] // IN_COL_TILE):
        sl = slice(j * IN_COL_TILE, (j + 1) * IN_COL_TILE)
        z_ref[:, sl] = _bf(_dot(hn, w_ref[:, sl]))


def _even_in_proj(h, gain, w_in):
    t, d = h.shape
    f = w_in.shape[1]
    tm = min(ROW_TILE, t)
    return pl.pallas_call(
        _even_in_kernel,
        out_shape=jax.ShapeDtypeStruct((t, f), BF16),
        grid=(t // tm,),
        in_specs=[
            pl.BlockSpec((tm, d), lambda i: (i, 0)),
            pl.BlockSpec((1, d), lambda i: (0, 0)),
            pl.BlockSpec((d, f), lambda i: (0, 0)),
        ],
        out_specs=pl.BlockSpec((tm, f), lambda i: (i, 0)),
        compiler_params=_params("parallel"),
        name="even_in_proj",
    )(h, gain.reshape(1, d), w_in)


def _ret_kernel(q_ref, k_ref, v_ref, g_ref, cos_ref, sin_ref, intra_ref, qdec_ref, kdec_ref, cdec_ref,
                gn_ref, o_ref, state_ref):
    @pl.when(pl.program_id(1) == 0)
    def _():
        state_ref[...] = jnp.zeros_like(state_ref)

    hd = RET_HEAD_DIM
    half = hd // 2
    cos = cos_ref[...]
    sin = sin_ref[...]
    hs = range(RET_HEADS)
    sl = [slice(h * hd, (h + 1) * hd) for h in hs]
    q = [q_ref[:, s].astype(F32) for s in sl]
    k = [k_ref[:, s].astype(F32) for s in sl]
    qr = [q[h] * cos + pltpu.roll(q[h], half, 1) * sin for h in hs]
    kr = [(k[h] * cos + pltpu.roll(k[h], half, 1) * sin) * (hd ** -0.5) for h in hs]
    v = [v_ref[:, s] for s in sl]
    scores = [_dot_nt(_bf(qr[h]), _bf(kr[h])) * intra_ref[h] for h in hs]
    state = [state_ref[h] for h in hs]
    out = [_dot(_bf(scores[h]), v[h]) + _dot(_bf(qr[h] * qdec_ref[h]), _bf(state[h])) for h in hs]
    for h in hs:
        state_ref[h] = state[h] * cdec_ref[h, 0:1, :] + _dot_tn(_bf(kr[h] * kdec_ref[h]), v[h])
    for h in hs:
        mu = jnp.mean(out[h], axis=-1, keepdims=True)
        dev = out[h] - mu
        var = jnp.mean(dev * dev, axis=-1, keepdims=True)
        y = dev * lax.rsqrt(var + GN_EPS) * gn_ref[:, sl[h]]
        o_ref[:, sl[h]] = _bf(y * _silu(g_ref[:, sl[h]].astype(F32)))


def _retention(z, batch, seq, gn_gain):
    t = z.shape[0]
    c = min(RET_CHUNK, seq)
    nc = seq // c
    hd = RET_HEAD_DIM
    half = hd // 2
    f32 = np.float32
    inv_freq = f32(ROPE_BASE) ** (-np.arange(half, dtype=f32) / f32(half))
    ang = np.arange(seq, dtype=f32)[:, None] * inv_freq[None, :]
    cos2 = np.concatenate([np.cos(ang), np.cos(ang)], axis=-1)
    sin2 = np.concatenate([-np.sin(ang), np.sin(ang)], axis=-1)
    lg = np.log(f32(1.0) - f32(2.0) ** (f32(-5.0) - np.arange(RET_HEADS, dtype=f32)))
    ci = np.arange(c, dtype=f32)
    diff = ci[:, None] - ci[None, :]
    intra = np.where(diff[None] >= 0, np.exp(np.maximum(diff, f32(0.0))[None] * lg[:, None, None]), f32(0.0))
    qdec = np.ascontiguousarray(np.broadcast_to(np.exp((ci + 1)[None, :, None] * lg[:, None, None]),
                                                (RET_HEADS, c, hd)))
    kdec = np.ascontiguousarray(np.broadcast_to(np.exp((c - 1 - ci)[None, :, None] * lg[:, None, None]),
                                                (RET_HEADS, c, hd)))
    cdec = np.ascontiguousarray(np.broadcast_to(np.exp(f32(c) * lg)[:, None, None], (RET_HEADS, SUBLANES, hd)))
    cos2, sin2, intra, qdec, kdec, cdec = (jnp.asarray(a, F32) for a in (cos2, sin2, intra, qdec, kdec, cdec))

    width = RET_HEADS * hd

    def zcol(j):
        return pl.BlockSpec((c, width), lambda b, n: (b * nc + n, j))

    full = lambda a: pl.BlockSpec(a.shape, lambda b, n: (0,) * a.ndim)
    gn = gn_gain.reshape(1, -1)
    return pl.pallas_call(
        _ret_kernel,
        out_shape=jax.ShapeDtypeStruct((t, width), BF16),
        grid=(batch, nc),
        in_specs=[
            zcol(0), zcol(1), zcol(2), zcol(3),
            pl.BlockSpec((c, hd), lambda b, n: (n, 0)),
            pl.BlockSpec((c, hd), lambda b, n: (n, 0)),
            full(intra), full(qdec), full(kdec), full(cdec), full(gn),
        ],
        out_specs=pl.BlockSpec((c, width), lambda b, n: (b * nc + n, 0)),
        scratch_shapes=[pltpu.VMEM((RET_HEADS, hd, hd), F32)],
        compiler_params=_params("parallel", "arbitrary"),
        name="retention",
    )(z, z, z, z, cos2, sin2, intra, qdec, kdec, cdec, gn)


def _sb_kernel(q_ref, k_ref, v_ref, g_ref, o_ref, acc_ref, run_ref, *, tile):
    i = pl.program_id(1)
    hd = SB_HEAD_DIM
    scale = hd ** -0.5
    hs = range(SB_HEADS)
    cols = [slice(h * hd, (h + 1) * hd) for h in hs]
    q = [q_ref[:, s] for s in cols]
    trow = lax.broadcasted_iota(jnp.int32, (tile, tile), 0)
    tcol = lax.broadcasted_iota(jnp.int32, (tile, tile), 1)
    tri = jnp.where(trow >= tcol, 1.0, 0.0).astype(BF16)
    causal = tcol < trow

    acc_ref[...] = jnp.zeros_like(acc_ref)
    run_ref[...] = jnp.zeros_like(run_ref)

    def span(kb, diagonal):
        ks = pl.multiple_of(kb * tile, tile)
        nz = [_dot_nt(q[h], k_ref[pl.ds(ks, tile), cols[h]]) * (-scale) for h in hs]
        l1mb = [jnp.minimum(nz[h], 0.0) - jnp.log(1.0 + jnp.exp2(jnp.abs(nz[h]) * (-LOG2E))) for h in hs]
        if diagonal:
            l1mb = [jnp.where(causal, l1mb[h], 0.0) for h in hs]
        cs = [_dot(_bf(l1mb[h]), tri) for h in hs]
        run = [run_ref[h] for h in hs]
        w = [jnp.exp(cs[h] + jnp.concatenate([run[h]] * (tile // LANES), axis=1) - nz[h]) for h in hs]
        if diagonal:
            w = [jnp.where(causal, w[h], 0.0) for h in hs]
        for h in hs:
            acc_ref[h] += _dot(_bf(w[h]), v_ref[pl.ds(ks, tile), cols[h]])
            run_ref[h] = run[h] + jnp.broadcast_to(cs[h][:, 0:1], run[h].shape)

    def alive():
        return (jnp.max(run_ref[...]) >= F32_EXP_UNDERFLOW_LOG).astype(jnp.int32)

    span(i, True)

    def body(carry):
        j, _ = carry
        span(i - 1 - j, False)
        return j + 1, alive()

    lax.while_loop(lambda c: (c[0] < i) & (c[1] > 0), body, (jnp.int32(0), alive()))
    for h in hs:
        o_ref[:, cols[h]] = _bf(acc_ref[h] * _silu(g_ref[:, cols[h]].astype(F32)))


def _stick_breaking(z, batch, seq):
    t = z.shape[0]
    width = SB_HEADS * SB_HEAD_DIM
    tile = min(SB_TILE, seq)
    nq = seq // tile
    base = 4
    return pl.pallas_call(
        functools.partial(_sb_kernel, tile=tile),
        out_shape=jax.ShapeDtypeStruct((t, width), BF16),
        grid=(batch, nq),
        in_specs=[
            pl.BlockSpec((tile, width), lambda b, i: (b * nq + i, base)),
            pl.BlockSpec((seq, width), lambda b, i: (b, base + 1)),
            pl.BlockSpec((seq, width), lambda b, i: (b, base + 2)),
            pl.BlockSpec((tile, width), lambda b, i: (b * nq + i, base + 3)),
        ],
        out_specs=pl.BlockSpec((tile, width), lambda b, i: (b * nq + i, 0)),
        scratch_shapes=[pltpu.VMEM((SB_HEADS, tile, SB_HEAD_DIM), F32), pltpu.VMEM((SB_HEADS, tile, LANES), F32)],
        compiler_params=_params("parallel", "arbitrary"),
        name="stick_breaking",
    )(z, z, z, z)


def _ple_tail(h1, p_ref, wg_ref, wp_ref, rest, final):
    gate = jax.nn.sigmoid(_dot(_bf(h1), wg_ref[...]))
    h2 = h1 + gate * _dot(_bf(p_ref[...]), wp_ref[...])
    if final:
        fg_ref, o_ref = rest
        o_ref[...] = _rms(h2, fg_ref[...])
    else:
        (o_ref,) = rest
        o_ref[...] = h2


def _even_out_kernel(h_ref, ma_ref, mb_ref, w_ref, p_ref, wg_ref, wp_ref, *rest, final):
    wa = ma_ref.shape[1]
    h1 = h_ref[...] + _dot(ma_ref[...], w_ref[:wa, :]) + _dot(mb_ref[...], w_ref[wa:, :])
    _ple_tail(h1, p_ref, wg_ref, wp_ref, rest, final)


def _odd_out_kernel(h_ref, y_ref, bonus_ref, gs_ref, lg_ref, lb_ref, w_ref, p_ref, wg_ref, wp_ref, *rest, final):
    ones_bd = _head_ones()
    inv_n = 1.0 / RWKV_HEAD_DIM
    y = y_ref[...]
    dev = y - _headsum(y, ones_bd) * inv_n
    var = _headsum(dev * dev, ones_bd) * inv_n
    yn = dev * lax.rsqrt(var + LNX_EPS) * lg_ref[...] + lb_ref[...]
    mixed = _bf((yn + bonus_ref[...]) * gs_ref[...].astype(F32))
    _ple_tail(h_ref[...] + _dot(mixed, w_ref[...]), p_ref, wg_ref, wp_ref, rest, final)


def _out_call(kern, name, h, row_args, full_args_mid, p, layer, w_gate, w_ple, final_gain):
    t, d = h.shape
    tm = min(OUT_ROW_TILE, t)
    final = final_gain is not None
    row = lambda a: pl.BlockSpec((tm, a.shape[1]), lambda i: (i, 0))
    full = lambda a: pl.BlockSpec(a.shape, lambda i: (0,) * a.ndim)
    args = [h, *row_args, *full_args_mid, p, w_gate, w_ple]
    p_spec = pl.BlockSpec((None, tm, p.shape[2]), lambda i: (layer, i, 0))
    specs = [row(h)] + [row(a) for a in row_args] + [full(a) for a in full_args_mid] + [p_spec, full(w_gate),
                                                                                        full(w_ple)]
    if final:
        args.append(final_gain.reshape(1, d))
        specs.append(full(args[-1]))
    return pl.pallas_call(
        functools.partial(kern, final=final),
        out_shape=jax.ShapeDtypeStruct((t, d), F32),
        grid=(t // tm,),
        in_specs=specs,
        out_specs=row(h),
        compiler_params=_params("parallel"),
        name=name,
    )(*args)


def _odd_in_kernel(*refs, vres, tiles_per_seq, chunk, sub):
    (h_ref, hp_ref, gain_ref, mu_ref, win_ref, w0_ref, w1_ref, w2_ref, a0_ref, a1_ref, a2_ref,
     kk_ref, ka_ref, rk_ref) = refs[:14]
    if vres:
        v0_ref, v1_ref, v2_ref, vf_ref = refs[14:18]
        outs = refs[18:]
    else:
        outs = refs[14:]
    rd_ref, kkd_ref, kinv_ref, binv_ref, kt_ref, bt_ref, vb_ref, wc_ref, bonus_ref, gs_ref = outs[:10]

    gain = gain_ref[...]
    hn_all = _rms(h_ref[...], gain)
    hnp = _rms(hp_ref[...], gain)
    tm = hn_all.shape[0]
    first = (pl.program_id(0) % tiles_per_seq) == 0
    prev_row = jnp.where(first, 0.0, hnp[SUBLANES - 1:SUBLANES, :])
    rowid = lax.broadcasted_iota(jnp.int32, hn_all.shape, 0)
    xx_all = jnp.where(rowid == 0, prev_row, pltpu.roll(hn_all, 1, 0)) - hn_all
    ones_bd = _head_ones()
    ka = ka_ref[...]
    ka_rest = 1.0 - ka
    ri = lax.broadcasted_iota(jnp.int32, (sub, sub), 0)
    ci = lax.broadcasted_iota(jnp.int32, (sub, sub), 1)
    sel = jnp.where(((ri // chunk) == (ci // chunk)) & (ri >= ci), 1.0, 0.0).astype(BF16)

    def part(r0):
        rows = slice(r0, r0 + sub)
        hn, xx = hn_all[rows], xx_all[rows]
        mix = lambda p: _bf(hn + xx * mu_ref[p:p + 1, :])
        r = _dot(mix(0), win_ref[0])
        yield
        k = _dot(mix(1), win_ref[1])
        yield
        xv = mix(2)
        v = _dot(xv, win_ref[2])
        yield
        g = _dot(mix(3), win_ref[3])
        yield
        xw = mix(4)
        xa = mix(5)
        w_in = w0_ref[...] + _dot(_bf(jnp.tanh(_dot(xw, w1_ref[...]))), w2_ref[...])
        lw = -DECAY_SCALE * jax.nn.sigmoid(w_in)
        a = jax.nn.sigmoid(a0_ref[...] + _dot(_bf(_dot(xa, a1_ref[...])), a2_ref[...]))
        if vres:
            mixv = jax.nn.sigmoid(v0_ref[...] + _dot(_bf(_dot(xv, v1_ref[...])), v2_ref[...]))
            v = v + (vf_ref[rows, :] - v) * mixv
        else:
            outs[10][rows, :] = v
        yield
        kkraw = k * kk_ref[...]
        kk = kkraw * lax.rsqrt(jnp.maximum(_headsum(kkraw * kkraw, ones_bd), 1e-24))
        kmod = k * (a * ka + ka_rest)
        b = kk * a
        hi, lo = _split(lw)
        cum = _dot(sel, hi) + _dot(sel, lo)
        ends = [jnp.exp(cum[(c + 1) * chunk - 1:(c + 1) * chunk, :]) for c in range(sub // chunk)]
        spread = lambda es: jnp.concatenate([jnp.broadcast_to(e, (chunk, e.shape[1])) for e in es], axis=0)
        wc, wc_neg = spread(ends), spread([-e for e in ends])
        yield
        w_inv = jnp.exp(-cum)
        kinv = kmod * w_inv
        binv = b * w_inv
        rd_ref[rows, :] = _bf(r * jnp.exp(cum))
        kkd_ref[rows, :] = _bf(kk * jnp.exp(cum - lw))
        kinv_ref[rows, :] = _bf(kinv)
        binv_ref[rows, :] = _bf(binv)
        yield
        kt_ref[rows, :] = _bf(kinv * wc)
        bt_ref[rows, :] = _bf(binv * wc_neg)
        vb_ref[rows, :] = _bf(v)
        for c in range(sub // chunk):
            wc_ref[r0 // chunk + c] = ends[c]
        bonus_ref[rows, :] = _headsum(r * kmod * rk_ref[...], ones_bd) * v
        gs_ref[rows, :] = _bf(_silu(g))
        yield

    parts = [part(r0) for r0 in range(0, tm, sub)]
    for _ in range(4):
        next(parts[0])
    for p in range(1, len(parts)):
        for _ in zip(parts[p - 1], parts[p]):
            pass
    for _ in parts[-1]:
        pass


def _odd_in_proj(h, seq, gain, mu, w_in, w0, w1, w2, a0, a1, a2, k_k, k_a, r_k, vres=None, v_first=None):
    t, d = h.shape
    tm = min(ODD_ROW_TILE, seq)
    chunk = SCAN_CHUNK
    tiles_per_seq = seq // tm
    sub_per_tile = tm // SUBLANES
    row = pl.BlockSpec((tm, d), lambda i: (i, 0))
    full = lambda a: pl.BlockSpec(a.shape, lambda i: (0,) * a.ndim)
    vec = lambda a: a.reshape(1, -1)
    args = [h, h, vec(gain), mu, w_in, vec(w0), w1, w2, vec(a0), a1, a2, vec(k_k), vec(k_a), vec(r_k)]
    specs = [row, pl.BlockSpec((SUBLANES, d), lambda i: (jnp.maximum(i * sub_per_tile - 1, 0), 0))]
    specs += [full(a) for a in args[2:]]
    if vres is not None:
        v0, v1, v2 = vres
        extra = [vec(v0), v1, v2]
        args += extra + [v_first]
        specs += [full(a) for a in extra] + [row]
    act = lambda dt: jax.ShapeDtypeStruct((t, d), dt)
    out_shape = [act(BF16)] * 7 + [jax.ShapeDtypeStruct((t // chunk, 1, d), F32), act(F32), act(BF16)]
    out_specs = [row] * 7 + [pl.BlockSpec((tm // chunk, 1, d), lambda i: (i, 0, 0)), row, row]
    if vres is None:
        out_shape.append(act(F32))
        out_specs.append(row)
    return pl.pallas_call(
        functools.partial(_odd_in_kernel, vres=vres is not None, tiles_per_seq=tiles_per_seq, chunk=chunk,
                          sub=min(ODD_SUB_ROWS, tm)),
        out_shape=out_shape,
        grid=(t // tm,),
        in_specs=specs,
        out_specs=out_specs,
        compiler_params=_params("parallel"),
        name="odd_in_proj",
    )(*args)


def _scan_kernel(rd_ref, kkd_ref, kinv_ref, binv_ref, kt_ref, bt_ref, v_ref, wc_ref, y_ref, state_ref, *, groups,
                 chunks):
    @pl.when(pl.program_id(0) == 0)
    def _():
        state_ref[...] = jnp.zeros_like(state_ref)

    batch = rd_ref.shape[0]
    c = rd_ref.shape[1] // chunks
    w = MXU_DIM
    hd = RWKV_HEAD_DIM
    row = lax.broadcasted_iota(jnp.int32, (c, w), 0)
    lane = lax.broadcasted_iota(jnp.int32, (c, w), 1)
    j = lane & (hd - 1)
    strict, incl, eye = j < row, j <= row, j == row
    blk16 = (j >> 4) == (row >> 4)
    blk32 = (j >> 5) == (row >> 5)
    head_of_lane = lane // hd
    head_masks = [head_of_lane == hh for hh in range(HEADS_PER_GROUP)]
    brow = lax.broadcasted_iota(jnp.int32, (w, w), 0)
    bcol = lax.broadcasted_iota(jnp.int32, (w, w), 1)
    bd = (brow // hd) == (bcol // hd)
    zero = jnp.zeros((), BF16)

    def stack(x):
        return jnp.concatenate([jnp.where(m, x, zero) for m in head_masks], axis=0)

    def pdot(x, y):
        return _dot(_bf(x), stack(_bf(y)))

    cols = [slice(g * w, (g + 1) * w) for g in range(groups)]
    rows = [slice(ch * c, (ch + 1) * c) for ch in range(chunks)]
    lanes = [(b, s) for b in range(batch) for s in cols]
    ix = [(b, r, s) for r in rows for b, s in lanes]
    ks = range(len(ix))
    rd = [rd_ref[i] for i in ix]
    kkd = [kkd_ref[i] for i in ix]
    v = [v_ref[i] for i in ix]
    lhs2 = [jnp.concatenate([kkd[k], rd[k]], axis=0) for k in ks]
    ak = [_dot_nt(lhs2[k], stack(kinv_ref[ix[k]])) for k in ks]
    ab = [_dot_nt(lhs2[k], stack(binv_ref[ix[k]])) for k in ks]
    a_kk = [_bf(jnp.where(strict, ak[k][:c], 0.0)) for k in ks]
    a_rk = [_bf(jnp.where(incl, ak[k][c:], 0.0)) for k in ks]
    n = [jnp.where(strict, ab[k][:c], 0.0) for k in ks]
    a_rb = [_bf(jnp.where(incl, ab[k][c:], 0.0)) for k in ks]

    nd = [jnp.where(blk16, n[k], 0.0) for k in ks]
    n2 = [pdot(nd[k], nd[k]) for k in ks]
    tinv = [jnp.where(eye, 1.0, 0.0) - nd[k] for k in ks]
    r1 = [pdot(jnp.concatenate([tinv[k], n2[k]], axis=0), n2[k]) for k in ks]
    tinv = [tinv[k] + r1[k][:c] for k in ks]
    r2 = [pdot(jnp.concatenate([tinv[k], r1[k][c:]], axis=0), r1[k][c:]) for k in ks]
    tinv = [tinv[k] + r2[k][:c] for k in ks]
    tinv = [tinv[k] + pdot(tinv[k], r2[k][c:]) for k in ks]
    off1 = [jnp.where(blk32 & jnp.logical_not(blk16), n[k], 0.0) for k in ks]
    m1 = [pdot(off1[k], tinv[k]) for k in ks]
    tinv = [tinv[k] - pdot(tinv[k], m1[k]) for k in ks]
    off2 = [jnp.where(blk32, 0.0, n[k]) for k in ks]
    m2 = [pdot(off2[k], tinv[k]) for k in ks]
    tinv = [_bf(tinv[k] - pdot(tinv[k], m2[k])) for k in ks]
    av = [_dot(jnp.concatenate([a_kk[k], a_rk[k]], axis=0), stack(v[k])) for k in ks]

    gs = range(len(lanes))
    state = [state_ref[g] for g in gs]
    for ch in range(chunks):
        chain = [ch * len(lanes) + g for g in gs]
        ps = [_dot_nt(lhs2[k], _bf(state[g])) for g, k in zip(gs, chain)]
        skb = [_bf(_dot(tinv[k], stack(_bf(ps[g][:c] + av[k][:c])))) for g, k in zip(gs, chain)]
        for g, k in zip(gs, chain):
            y_ref[ix[k]] = ps[g][c:] + av[k][c:] - _dot(a_rb[k], stack(skb[g]))
        upd = [_dot_tn(jnp.concatenate([v[k], skb[g]], axis=0),
                       jnp.concatenate([kt_ref[ix[k]], bt_ref[ix[k]]], axis=0))
               for g, k in zip(gs, chain)]
        state = [state[g] * wc_ref[lanes[g][0], ch, :, lanes[g][1]] + jnp.where(bd, upd[g], 0.0) for g in gs]
    for g in gs:
        state_ref[g] = state[g]


def _rwkv_scan(rd, kkd, kinv, binv, ktail, btail, vb, wc, batch, seq):
    t, d = rd.shape
    chunks = SCAN_CHUNKS_PER_STEP
    rows = SCAN_CHUNK * chunks
    groups = d // MXU_DIM
    blk = pl.BlockSpec((batch, rows, d), lambda n: (0, n, 0))
    seqs = [a.reshape(batch, seq, d) for a in (rd, kkd, kinv, binv, ktail, btail, vb)]
    y = pl.pallas_call(
        functools.partial(_scan_kernel, groups=groups, chunks=chunks),
        out_shape=jax.ShapeDtypeStruct((batch, seq, d), F32),
        grid=(seq // rows,),
        in_specs=[blk] * 7 + [pl.BlockSpec((batch, chunks, 1, d), lambda n: (0, n, 0, 0))],
        out_specs=blk,
        scratch_shapes=[pltpu.VMEM((batch * groups, MXU_DIM, MXU_DIM), F32)],
        compiler_params=_params("arbitrary"),
        name="rwkv7_scan",
    )(*seqs, wc.reshape(batch, seq // SCAN_CHUNK, 1, d))
    return y.reshape(t, d)


def kernel(x, p, norm_gain, final_gain, ple_proj, ple_gate, even_w_in, even_w_out, ret_gn_gain, odd_mu, odd_w_in, odd_w_out, rwkv_w0, rwkv_w1, rwkv_w2, rwkv_a0, rwkv_a1, rwkv_a2, rwkv_v0, rwkv_v1, rwkv_v2, rwkv_k_k, rwkv_k_a, rwkv_r_k, rwkv_lnx_gain, rwkv_lnx_bias):
    batch, seq, d = x.shape
    depth = p.shape[0]
    t = batch * seq
    h = x.reshape(t, d)
    p2 = p.reshape(depth, t, p.shape[-1])
    vec = lambda a: a.reshape(1, -1)
    v_first = None
    for i in range(depth):
        final = final_gain if i == depth - 1 else None
        w_gate, w_ple = _bf(ple_gate[i]), _bf(ple_proj[i])
        if i % 2 == 0:
            e = i // 2
            z = _even_in_proj(h, norm_gain[i], _bf(even_w_in[e]))
            mixed_a = _retention(z, batch, seq, ret_gn_gain[e])
            mixed_b = _stick_breaking(z, batch, seq)
            h = _out_call(_even_out_kernel, "even_out_proj_ple", h, [mixed_a, mixed_b], [_bf(even_w_out[e])],
                          p2, i, w_gate, w_ple, final)
        else:
            o = i // 2
            vres = None if v_first is None else (rwkv_v0[o - 1], _bf(rwkv_v1[o - 1]), _bf(rwkv_v2[o - 1]))
            outs = _odd_in_proj(
                h, seq, norm_gain[i], odd_mu[o], _bf(odd_w_in[o]), rwkv_w0[o], _bf(rwkv_w1[o]), _bf(rwkv_w2[o]),
                rwkv_a0[o], _bf(rwkv_a1[o]), _bf(rwkv_a2[o]), rwkv_k_k[o], rwkv_k_a[o], rwkv_r_k[o], vres, v_first)
            rd, kkd, kinv, binv, ktail, btail, vb, wc, bonus, gs = outs[:10]
            if v_first is None:
                v_first = outs[10]
            y = _rwkv_scan(rd, kkd, kinv, binv, ktail, btail, vb, wc, batch, seq)
            h = _out_call(_odd_out_kernel, "odd_out_proj_ple", h, [y, bonus, gs],
                          [vec(rwkv_lnx_gain[o]), vec(rwkv_lnx_bias[o]), _bf(odd_w_out[o])], p2, i, w_gate, w_ple,
                          final)
    return h.reshape(batch, seq, d)
```
